```python
import math
import jax, jax.numpy as jnp
from jax import lax
import numpy as np

D_MODEL = 1024
BATCH = 2
SEQ = 8192
DEPTH = 2
DEC_BATCH = 128
DEC_SEQ = 4
PAST_LEN = 2048
PAGE_SIZE = 128

HEAD_DIM = 64
CONV_WIDTH = D_MODEL // 2
CONV_K = 3
H_B = (D_MODEL // 2) // HEAD_DIM
SUB_WINDOW = 128
DILATIONS = (1, 4, 16)
WINDOWS = tuple(SUB_WINDOW * d for d in DILATIONS)
MAX_WINDOW = max(WINDOWS)
H_C = D_MODEL // (2 * HEAD_DIM)
D_FF = 256 * ((8 * D_MODEL // 3 + 255) // 256)
EVEN_SPLIT = (CONV_WIDTH,) * 3 + (H_B * HEAD_DIM,) * 3
EVEN_IN = sum(EVEN_SPLIT)
EVEN_MIX = CONV_WIDTH + H_B * HEAD_DIM
QC = H_C * 2 * HEAD_DIM
ODD_IN = 3 * QC
ODD_MIX = QC
N_EVEN = (DEPTH + 1) // 2
N_ODD = DEPTH // 2
Q_BLOCK = 128
ATTN_SCALE = HEAD_DIM ** -0.5
RMS_EPS = 1e-6
NEG_INF = -1e30

kernel_name = 'hybrid_conv_dilated_diffattn_step'


def rmsnorm(x, g):
    xf = x.astype(jnp.float32)
    y = xf * lax.rsqrt(jnp.mean(xf * xf, axis=-1, keepdims=True) + RMS_EPS)
    return (y * g.astype(jnp.float32)).astype(x.dtype)


def alibi_slopes(n_heads):
    return jnp.asarray([2.0 ** (-8.0 * (h + 1) / n_heads) for h in range(n_heads)], dtype=jnp.float32)


def swiglu(x, wg, wu, wd):
    return (jax.nn.silu(x @ wg) * (x @ wu)) @ wd


def half_ffn(x, g, wg, wu, wd):
    return x + 0.5 * swiglu(rmsnorm(x, g), wg, wu, wd)


def split_cols(a, sizes):
    return jnp.split(a, np.cumsum(sizes)[:-1].tolist(), axis=-1)


def causal_conv3(u, prefix, w):
    t = u.shape[1]
    cat = jnp.concatenate([prefix.astype(u.dtype), u], axis=1)
    y = sum(w[j] * cat[:, j:j + t] for j in range(CONV_K))
    return y, cat[:, cat.shape[1] - (CONV_K - 1):]


def last_rows(a, n):
    t = a.shape[1]
    a = jnp.pad(a, ((0, 0), (max(n - t, 0), 0)) + ((0, 0),) * (a.ndim - 2))
    return a[:, a.shape[1] - n:]


def even_inputs(h, w_in, qk_g):
    n, t, _ = h.shape
    b_gate, c_gate, xa, q, k, v = split_cols(h @ w_in, EVEN_SPLIT)
    shp = (n, t, H_B, HEAD_DIM)
    q = rmsnorm(q.reshape(shp), qk_g[0])
    k = rmsnorm(k.reshape(shp), qk_g[1])
    return b_gate, c_gate * xa, q, k, v.reshape(shp)


def dilated_branch_prompt(q, k, v, d, slopes):
    n, s_len, h, hd = q.shape
    unit = d * Q_BLOCK
    s_pad = -(-s_len // unit) * unit
    l_sub = s_pad // d
    nb = l_sub // Q_BLOCK

    def to_residue_blocks(a):
        a = jnp.pad(a, ((0, 0), (0, s_pad - s_len), (0, 0), (0, 0)))
        a = a.reshape(n, l_sub, d, h, hd).transpose(0, 2, 1, 3, 4)
        return a.reshape(n * d, nb, Q_BLOCK, h, hd)

    def with_prev_block(a):
        prev = jnp.pad(a[:, :-1], ((0, 0), (1, 0), (0, 0), (0, 0), (0, 0)))
        return jnp.concatenate([prev, a], axis=2)

    qr = to_residue_blocks(q)
    kb = with_prev_block(to_residue_blocks(k))
    vb = with_prev_block(to_residue_blocks(v))
    qi = jnp.arange(Q_BLOCK)[:, None]
    kj = jnp.arange(2 * Q_BLOCK)[None, :]
    dist = qi + Q_BLOCK - kj
    band = (dist >= 0) & (dist <= SUB_WINDOW)
    no_prev = (jnp.arange(nb) == 0)[:, None, None] & (kj < Q_BLOCK)[None]
    mask = band[None] & ~no_prev
    bias = -slopes[:, None, None] * (d * dist).astype(jnp.float32)[None]
    s = jnp.einsum('nbqhd,nbkhd->nbhqk', qr, kb).astype(jnp.float32) * ATTN_SCALE + bias[None, None]
    s = jnp.where(mask[None, :, None], s, NEG_INF)
    lse = jax.nn.logsumexp(s, axis=-1)
    p = jnp.exp(s - lse[..., None]).astype(v.dtype)
    o = jnp.einsum('nbhqk,nbkhd->nbqhd', p, vb)
    o = o.reshape(n, d, l_sub, h, hd).transpose(0, 2, 1, 3, 4).reshape(n, s_pad, h, hd)[:, :s_len]
    lse = lse.transpose(0, 1, 3, 2).reshape(n, d, l_sub, h).transpose(0, 2, 1, 3).reshape(n, s_pad, h)[:, :s_len]
    return o, lse


def dilated_branch_sample(q, kcat, vcat, d, slopes, buf_len):
    t = q.shape[1]
    steps = jnp.arange(SUB_WINDOW + 1)
    idx = buf_len + jnp.arange(t)[:, None] - d * steps[None, :]
    valid = idx >= 0
    idxc = jnp.maximum(idx, 0)
    kg = kcat[:, idxc]
    vg = vcat[:, idxc]
    bias = -slopes[:, None, None] * (d * steps).astype(jnp.float32)[None, None, :]
    s = jnp.einsum('nthd,ntjhd->nhtj', q, kg).astype(jnp.float32) * ATTN_SCALE + bias[None]
    s = jnp.where(valid[None, None], s, NEG_INF)
    lse = jax.nn.logsumexp(s, axis=-1)
    p = jnp.exp(s - lse[..., None]).astype(vcat.dtype)
    o = jnp.einsum('nhtj,ntjhd->nthd', p, vg)
    return o, lse.transpose(0, 2, 1)


def combine_by_denominator(outs, lses):
    wts = jax.nn.softmax(jnp.stack(lses, axis=0), axis=0)
    o = sum(wts[i][..., None] * outs[i].astype(jnp.float32) for i in range(len(outs)))
    return o.astype(outs[0].dtype)


def even_output(a_out, outs, lses, w_out):
    n, t, _ = a_out.shape
    b_out = combine_by_denominator(outs, lses).reshape(n, t, H_B * HEAD_DIM)
    return jnp.concatenate([a_out, b_out], axis=-1) @ w_out


def even_mixer_prompt(h, w_in, conv_w, qk_g, w_out, buf_len):
    n = h.shape[0]
    b_gate, u, q, k, v = even_inputs(h, w_in, qk_g)
    y_conv, conv_state = causal_conv3(u, jnp.zeros((n, CONV_K - 1, CONV_WIDTH), u.dtype), conv_w)
    slopes = alibi_slopes(H_B)
    res = [dilated_branch_prompt(q, k, v, d, slopes) for d in DILATIONS]
    out = even_output(b_gate * y_conv, [r[0] for r in res], [r[1] for r in res], w_out)
    return out, conv_state, last_rows(k, buf_len), last_rows(v, buf_len)


def even_mixer_sample(h, conv_state, kbuf, vbuf, w_in, conv_w, qk_g, w_out):
    b_gate, u, q, k, v = even_inputs(h, w_in, qk_g)
    y_conv, new_conv = causal_conv3(u, conv_state, conv_w)
    buf_len = kbuf.shape[1]
    kcat = jnp.concatenate([kbuf.astype(k.dtype), k], axis=1)
    vcat = jnp.concatenate([vbuf.astype(v.dtype), v], axis=1)
    slopes = alibi_slopes(H_B)
    res = [dilated_branch_sample(q, kcat, vcat, d, slopes, buf_len) for d in DILATIONS]
    out = even_output(b_gate * y_conv, [r[0] for r in res], [r[1] for r in res], w_out)
    return out, new_conv, k, v


def odd_inputs(h, w_in, qk_g):
    n, t, _ = h.shape
    q, k, v = split_cols(h @ w_in, (QC, QC, QC))
    q = rmsnorm(q.reshape(n, t, H_C, 2, HEAD_DIM), qk_g[0])
    k = rmsnorm(k.reshape(n, t, H_C, 2, HEAD_DIM), qk_g[1])
    return q, k, v.reshape(n, t, H_C, 2 * HEAD_DIM)


def diff_lambda(lam, layer):
    lam_init = 0.8 - 0.6 * math.exp(-0.3 * layer)
    lf = lam.astype(jnp.float32)
    lam_v = jnp.exp(jnp.sum(lf[0] * lf[1])) - jnp.exp(jnp.sum(lf[2] * lf[3])) + lam_init
    return lam_v, lam_init


def diff_attend(q, q_pos, k, v, k_pos, lam_v, slopes):
    dist = (q_pos[:, None] - k_pos[None, :]).astype(jnp.float32)
    s = jnp.einsum('nqhcd,nkhcd->nhcqk', q, k).astype(jnp.float32) * ATTN_SCALE
    s = s - slopes[:, None, None, None] * dist[None, None]
    s = jnp.where(dist >= 0, s, NEG_INF)
    p = jax.nn.softmax(s, axis=-1)
    a = p[:, :, 0] - lam_v * p[:, :, 1]
    return jnp.einsum('nhqk,nkhe->nqhe', a.astype(v.dtype), v)


def odd_output(o, subln, lam_init, w_out):
    n, t = o.shape[:2]
    o = rmsnorm(o, subln) * (1.0 - lam_init)
    return o.reshape(n, t, ODD_MIX) @ w_out


def odd_mixer_prompt(h, w_in, qk_g, lam, subln, w_out, layer):
    n, t, _ = h.shape
    q, k, v = odd_inputs(h, w_in, qk_g)
    lam_v, lam_init = diff_lambda(lam, layer)
    slopes = alibi_slopes(H_C)
    k_pos = jnp.arange(t)
    nb = t // Q_BLOCK
    qb = q.reshape(n, nb, Q_BLOCK, H_C, 2, HEAD_DIM).transpose(1, 0, 2, 3, 4, 5)

    def block(args):
        q_blk, b = args
        q_pos = b * Q_BLOCK + jnp.arange(Q_BLOCK)
        return diff_attend(q_blk, q_pos, k, v, k_pos, lam_v, slopes)

    o = lax.map(block, (qb, jnp.arange(nb)))
    o = o.transpose(1, 0, 2, 3, 4).reshape(n, t, H_C, 2 * HEAD_DIM)
    out = odd_output(o, subln, lam_init, w_out)
    k_pages = k.reshape(n, t // PAGE_SIZE, PAGE_SIZE, H_C, 2 * HEAD_DIM)
    v_pages = v.reshape(n, t // PAGE_SIZE, PAGE_SIZE, H_C, 2 * HEAD_DIM)
    return out, k_pages, v_pages


def odd_mixer_sample(h, cache_k, cache_v, page_table, j, w_in, qk_g, lam, subln, w_out, layer):
    n, t, _ = h.shape
    q, k, v = odd_inputs(h, w_in, qk_g)
    lam_v, lam_init = diff_lambda(lam, layer)
    past = page_table.shape[1] * PAGE_SIZE
    past_k = cache_k[j, page_table].reshape(n, past, H_C, 2, HEAD_DIM)
    past_v = cache_v[j, page_table].reshape(n, past, H_C, 2 * HEAD_DIM)
    kcat = jnp.concatenate([past_k.astype(k.dtype), k], axis=1)
    vcat = jnp.concatenate([past_v.astype(v.dtype), v], axis=1)
    q_pos = past + jnp.arange(t)
    o = diff_attend(q, q_pos, kcat, vcat, jnp.arange(past + t), lam_v, alibi_slopes(H_C))
    out = odd_output(o, subln, lam_init, w_out)
    return out, k.reshape(n, t, H_C, 2 * HEAD_DIM), v


def setup_inputs(seed: int = 0) -> dict:
    key = jax.random.key(seed)
    ks = jax.random.split(key, 24)
    f32 = jnp.float32
    n_pages = PAST_LEN // PAGE_SIZE
    n_used = DEC_BATCH * n_pages
    n_phys = n_used + n_used // 4
    w_buf = min(MAX_WINDOW, PAST_LEN)

    def nrm(k, shape, scale=1.0):
        return jax.random.normal(k, shape, f32) * scale

    page_table = jax.random.permutation(ks[7], n_phys)[:n_used].reshape(DEC_BATCH, n_pages).astype(jnp.int32)
    return {
        'x_prompt': nrm(ks[0], (BATCH, SEQ, D_MODEL)),
        'x_sample': nrm(ks[1], (DEC_BATCH, DEC_SEQ, D_MODEL)),
        'state_conv': nrm(ks[2], (N_EVEN, DEC_BATCH, CONV_K - 1, CONV_WIDTH)),
        'cache_b_k': nrm(ks[3], (N_EVEN, DEC_BATCH, w_buf, H_B, HEAD_DIM)),
        'cache_b_v': nrm(ks[4], (N_EVEN, DEC_BATCH, w_buf, H_B, HEAD_DIM)),
        'cache_c_k': nrm(ks[5], (N_ODD, n_phys, PAGE_SIZE, H_C, 2 * HEAD_DIM)),
        'cache_c_v': nrm(ks[6], (N_ODD, n_phys, PAGE_SIZE, H_C, 2 * HEAD_DIM)),
        'page_table': page_table,
        'norm_gain': 1.0 + 0.01 * nrm(ks[8], (DEPTH, 3, D_MODEL)),
        'ffn_w_gate': nrm(ks[9], (DEPTH, 2, D_MODEL, D_FF), D_MODEL ** -0.5),
        'ffn_w_up': nrm(ks[10], (DEPTH, 2, D_MODEL, D_FF), D_MODEL ** -0.5),
        'ffn_w_down': nrm(ks[11], (DEPTH, 2, D_FF, D_MODEL), D_FF ** -0.5),
        'w_in_even': nrm(ks[12], (N_EVEN, D_MODEL, EVEN_IN), D_MODEL ** -0.5),
        'conv_w': nrm(ks[13], (N_EVEN, CONV_K, CONV_WIDTH), CONV_K ** -0.5),
        'qk_gain_b': 1.0 + 0.01 * nrm(ks[14], (N_EVEN, 2, HEAD_DIM)),
        'w_out_even': nrm(ks[15], (N_EVEN, EVEN_MIX, D_MODEL), EVEN_MIX ** -0.5),
        'w_in_odd': nrm(ks[16], (N_ODD, D_MODEL, ODD_IN), D_MODEL ** -0.5),
        'qk_gain_c': 1.0 + 0.01 * nrm(ks[17], (N_ODD, 2, HEAD_DIM)),
        'lambda_c': nrm(ks[18], (N_ODD, 4, HEAD_DIM), 0.1),
        'subln_c': 1.0 + 0.01 * nrm(ks[19], (N_ODD, 2 * HEAD_DIM)),
        'w_out_odd': nrm(ks[20], (N_ODD, ODD_MIX, D_MODEL), ODD_MIX ** -0.5),
    }


def reference(x_prompt, x_sample, state_conv, cache_b_k, cache_b_v, cache_c_k, cache_c_v, page_table,
              norm_gain, ffn_w_gate, ffn_w_up, ffn_w_down, w_in_even, conv_w, qk_gain_b, w_out_even,
              w_in_odd, qk_gain_c, lambda_c, subln_c, w_out_odd):
    xp, xs = x_prompt, x_sample
    buf_len = cache_b_k.shape[2]
    conv_p, conv_s, bk_p, bv_p, bk_s, bv_s = [], [], [], [], [], []
    ck_p, cv_p, ck_s, cv_s = [], [], [], []
    for layer in range(DEPTH):
        g = norm_gain[layer]
        xp = half_ffn(xp, g[0], ffn_w_gate[layer, 0], ffn_w_up[layer, 0], ffn_w_down[layer, 0])
        xs = half_ffn(xs, g[0], ffn_w_gate[layer, 0], ffn_w_up[layer, 0], ffn_w_down[layer, 0])
        hp, hs = rmsnorm(xp, g[1]), rmsnorm(xs, g[1])
        j = layer // 2
        if layer % 2 == 0:
            mp, c_p, k_p, v_p = even_mixer_prompt(hp, w_in_even[j], conv_w[j], qk_gain_b[j], w_out_even[j], buf_len)
            ms, c_s, k_s, v_s = even_mixer_sample(hs, state_conv[j], cache_b_k[j], cache_b_v[j],
                                                  w_in_even[j], conv_w[j], qk_gain_b[j], w_out_even[j])
            conv_p.append(c_p); conv_s.append(c_s)
            bk_p.append(k_p); bv_p.append(v_p); bk_s.append(k_s); bv_s.append(v_s)
        else:
            mp, k_p, v_p = odd_mixer_prompt(hp, w_in_odd[j], qk_gain_c[j], lambda_c[j], subln_c[j], w_out_odd[j], layer)
            ms, k_s, v_s = odd_mixer_sample(hs, cache_c_k, cache_c_v, page_table, j, w_in_odd[j], qk_gain_c[j],
                                            lambda_c[j], subln_c[j], w_out_odd[j], layer)
            ck_p.append(k_p); cv_p.append(v_p); ck_s.append(k_s); cv_s.append(v_s)
        xp = xp + mp
        xs = xs + ms
        xp = half_ffn(xp, g[2], ffn_w_gate[layer, 1], ffn_w_up[layer, 1], ffn_w_down[layer, 1])
        xs = half_ffn(xs, g[2], ffn_w_gate[layer, 1], ffn_w_up[layer, 1], ffn_w_down[layer, 1])
    return (xp, xs,
            jnp.stack(conv_p), jnp.stack(conv_s),
            jnp.stack(bk_p), jnp.stack(bv_p), jnp.stack(bk_s), jnp.stack(bv_s),
            jnp.stack(ck_p), jnp.stack(cv_p), jnp.stack(ck_s), jnp.stack(cv_s))
```

```python
import functools
import math

import jax
import jax.numpy as jnp
from jax import lax
from jax.experimental import pallas as pl
from jax.experimental.pallas import tpu as pltpu

F32 = jnp.float32
BF16 = jnp.bfloat16

HEAD_DIM = 64
CONV_K = 3
SUB_WINDOW = 128
DILATIONS = (1, 4, 16)
PAGE_SIZE = 128
ATTN_SCALE = HEAD_DIM ** -0.5
LOG2E = math.log2(math.e)
RMS_EPS = 1e-6
NEG_INF = -1e30

V7X_VMEM_LIMIT_BYTES = 56 * 1024 * 1024
MXU_COLS = 256
LANES = 128


def _params(*sem):
    return pltpu.CompilerParams(dimension_semantics=sem, vmem_limit_bytes=V7X_VMEM_LIMIT_BYTES)


def _rms_rows(x):
    return lax.rsqrt(jnp.mean(x * x, axis=-1, keepdims=True) + RMS_EPS)


def _group_ones(width, group):
    i = jnp.arange(width) // group
    return (i[:, None] == i[None, :]).astype(BF16)


def _group_rmsnorm(x, bd_ref, gain):
    x2 = x * x
    hi = x2.astype(BF16)
    lo = (x2 - hi.astype(F32)).astype(BF16)
    bd = bd_ref[...]
    cols = []
    for c in range(x.shape[1] // MXU_COLS):
        sl = slice(c * MXU_COLS, (c + 1) * MXU_COLS)
        cols.append(jnp.dot(hi[:, sl], bd, preferred_element_type=F32)
                    + jnp.dot(lo[:, sl], bd, preferred_element_type=F32))
    ss = jnp.concatenate(cols, axis=1)
    return x * lax.rsqrt(ss * (1.0 / HEAD_DIM) + RMS_EPS) * gain


def _ffn_kernel(x_ref, g_ref, wg_ref, wu_ref, wd_ref, o_ref, h_ref, acc_ref):
    j = pl.program_id(1)

    @pl.when(j == 0)
    def _():
        x = x_ref[...]
        h_ref[...] = (x * _rms_rows(x) * g_ref[...]).astype(BF16)
        acc_ref[...] = jnp.zeros_like(acc_ref)

    h = h_ref[...]
    a = jnp.dot(h, wg_ref[...], preferred_element_type=F32)
    b = jnp.dot(h, wu_ref[...], preferred_element_type=F32)
    z = (a * (1.0 / (1.0 + jnp.exp(-a))) * b).astype(BF16)
    acc_ref[...] += jnp.dot(z, wd_ref[...], preferred_element_type=F32)

    @pl.when(j == pl.num_programs(1) - 1)
    def _():
        o_ref[...] = x_ref[...] + 0.5 * acc_ref[...]


def _half_ffn(x, g, wg, wu, wd, *, tm, tf=MXU_COLS):
    t, d = x.shape
    f = wg.shape[1]
    return pl.pallas_call(
        _ffn_kernel,
        out_shape=jax.ShapeDtypeStruct((t, d), F32),
        grid=(t // tm, f // tf),
        in_specs=[
            pl.BlockSpec((tm, d), lambda i, j: (i, 0)),
            pl.BlockSpec((1, d), lambda i, j: (0, 0)),
            pl.BlockSpec((d, tf), lambda i, j: (0, j)),
            pl.BlockSpec((d, tf), lambda i, j: (0, j)),
            pl.BlockSpec((tf, d), lambda i, j: (j, 0)),
        ],
        out_specs=pl.BlockSpec((tm, d), lambda i, j: (i, 0)),
        scratch_shapes=[pltpu.VMEM((tm, d), BF16), pltpu.VMEM((tm, d), F32)],
        compiler_params=_params("parallel", "arbitrary"),
        name="half_ffn",
    )(x, g.reshape(1, d), wg, wu, wd)


def _inproj_even_kernel(x_ref, g_ref, w_ref, bd_ref, gq_ref, gk_ref,
                        bg_ref, u_ref, qb_ref, kb_ref, vb_ref, kf_ref, vf_ref):
    x = x_ref[...]
    h = (x * _rms_rows(x) * g_ref[...]).astype(BF16)
    cw = u_ref.shape[1]

    def col(c):
        return jnp.dot(h, w_ref[:, c * cw:(c + 1) * cw], preferred_element_type=F32)

    bg_ref[...] = col(0)
    u_ref[...] = col(1) * col(2)
    q = _group_rmsnorm(col(3), bd_ref, gq_ref[...])
    k = _group_rmsnorm(col(4), bd_ref, gk_ref[...])
    v = col(5)
    qb_ref[...] = (q * ATTN_SCALE).astype(BF16)
    kb_ref[...] = k.astype(BF16)
    vb_ref[...] = v.astype(BF16)
    kf_ref[...] = k
    vf_ref[...] = v


def _inproj_even(x, g, w, qk_g, *, tm):
    t, d = x.shape
    cw = w.shape[1] // 6
    reps = cw // HEAD_DIM
    row = lambda i: (i, 0)
    fix = lambda i: (0, 0)
    outs = [jax.ShapeDtypeStruct((t, cw), F32)] * 2 + [jax.ShapeDtypeStruct((t, cw), BF16)] * 3 \
        + [jax.ShapeDtypeStruct((t, cw), F32)] * 2
    return pl.pallas_call(
        _inproj_even_kernel,
        out_shape=outs,
        grid=(t // tm,),
        in_specs=[
            pl.BlockSpec((tm, d), row),
            pl.BlockSpec((1, d), fix),
            pl.BlockSpec(w.shape, fix),
            pl.BlockSpec((MXU_COLS, MXU_COLS), fix),
            pl.BlockSpec((1, cw), fix),
            pl.BlockSpec((1, cw), fix),
        ],
        out_specs=[pl.BlockSpec((tm, cw), row)] * 7,
        compiler_params=_params("parallel"),
        name="inproj_even",
    )(x, g.reshape(1, d), w, _group_ones(MXU_COLS, HEAD_DIM),
      jnp.tile(qk_g[0], reps).reshape(1, cw), jnp.tile(qk_g[1], reps).reshape(1, cw))


def _inproj_odd_kernel(x_ref, g_ref, w_ref, bd_ref, gq_ref, gk_ref, *rest, kv_block):
    if kv_block:
        wvt_ref, qb_ref, kb_ref, kf_ref, vf_ref, vt_ref = rest
    else:
        qb_ref, kb_ref, kf_ref, vf_ref = rest
    x = x_ref[...]
    h = (x * _rms_rows(x) * g_ref[...]).astype(BF16)
    cw = qb_ref.shape[1]

    def col(c):
        return jnp.dot(h, w_ref[:, c * cw:(c + 1) * cw], preferred_element_type=F32)

    q = _group_rmsnorm(col(0), bd_ref, gq_ref[...])
    k = _group_rmsnorm(col(1), bd_ref, gk_ref[...])
    qb_ref[...] = (q * (ATTN_SCALE * LOG2E)).astype(BF16)
    kb_ref[...] = k.astype(BF16)
    kf_ref[...] = k
    vf_ref[...] = col(2)
    if kv_block:
        vt = lax.dot_general(wvt_ref[...], h, (((1,), (1,)), ((), ())), preferred_element_type=F32).astype(BF16)
        for c in range(vt_ref.shape[0]):
            vt_ref[c] = vt[:, c * kv_block:(c + 1) * kv_block]


def _inproj_odd(x, g, w, qk_g, *, tm, kv_block=0):
    t, d = x.shape
    cw = w.shape[1] // 3
    reps = cw // HEAD_DIM
    row = lambda i: (i, 0)
    fix = lambda i: (0, 0)
    ins = [x, g.reshape(1, d), w, _group_ones(MXU_COLS, HEAD_DIM),
           jnp.tile(qk_g[0], reps).reshape(1, cw), jnp.tile(qk_g[1], reps).reshape(1, cw)]
    in_specs = [pl.BlockSpec((tm, d), row), pl.BlockSpec((1, d), fix), pl.BlockSpec(w.shape, fix),
                pl.BlockSpec((MXU_COLS, MXU_COLS), fix), pl.BlockSpec((1, cw), fix), pl.BlockSpec((1, cw), fix)]
    outs = [jax.ShapeDtypeStruct((t, cw), BF16)] * 2 + [jax.ShapeDtypeStruct((t, cw), F32)] * 2
    out_specs = [pl.BlockSpec((tm, cw), row)] * 4
    if kv_block:
        ins.append(w[:, 2 * cw:].T)
        in_specs.append(pl.BlockSpec((cw, d), fix))
        outs.append(jax.ShapeDtypeStruct((t // kv_block, cw, kv_block), BF16))
        out_specs.append(pl.BlockSpec((tm // kv_block, cw, kv_block), lambda i: (i, 0, 0)))
    return pl.pallas_call(
        functools.partial(_inproj_odd_kernel, kv_block=kv_block),
        out_shape=outs,
        grid=(t // tm,),
        in_specs=in_specs,
        out_specs=out_specs,
        compiler_params=_params("parallel"),
        name="inproj_odd",
    )(*ins)


def _outproj_kernel(x_ref, a_ref, w_ref, o_ref):
    o_ref[...] = x_ref[...] + jnp.dot(a_ref[...].astype(BF16), w_ref[...], preferred_element_type=F32)


def _outproj(x, a, w, *, tm):
    t, d = x.shape
    row = lambda i: (i, 0)
    return pl.pallas_call(
        _outproj_kernel,
        out_shape=jax.ShapeDtypeStruct((t, d), F32),
        grid=(t // tm,),
        in_specs=[pl.BlockSpec((tm, d), row), pl.BlockSpec((tm, a.shape[1]), row),
                  pl.BlockSpec(w.shape, lambda i: (0, 0))],
        out_specs=pl.BlockSpec((tm, d), row),
        compiler_params=_params("parallel"),
        name="outproj",
    )(x, a, w)


def _alibi_slope(h, n_heads):
    return 2.0 ** (-8.0 * (h + 1) / n_heads)


def _dilated_prompt_kernel(q_ref, kc_ref, kp_ref, vc_ref, vp_ref, o_ref, lse_ref, *, dilation, n_heads):
    blk = pl.program_id(2)
    tq = q_ref.shape[1]
    qi = lax.broadcasted_iota(jnp.int32, (tq, 2 * tq), 0)
    kj = lax.broadcasted_iota(jnp.int32, (tq, 2 * tq), 1)
    dist = qi + tq - kj
    valid = (dist >= 0) & (dist <= SUB_WINDOW) & ((kj >= tq) | (blk > 0))
    ndist = (-dilation * dist).astype(F32)
    lane_hi = lax.broadcasted_iota(jnp.int32, (1, LANES), 1) >= HEAD_DIM
    for p in range(n_heads // 2):
        sl = slice(p * LANES, (p + 1) * LANES)
        q = q_ref[0, :, sl]
        k = jnp.concatenate([kp_ref[0, :, sl], kc_ref[0, :, sl]], axis=0)
        v = jnp.concatenate([vp_ref[0, :, sl], vc_ref[0, :, sl]], axis=0)
        o_pair = jnp.zeros((tq, LANES), F32)
        lse_pair = jnp.zeros((tq, LANES), F32)
        for e in range(2):
            keep = lane_hi if e else jnp.logical_not(lane_hi)
            qe = jnp.where(keep, q, jnp.zeros_like(q))
            ve = jnp.where(keep, v, jnp.zeros_like(v))
            s = lax.dot_general(qe, k, (((1,), (1,)), ((), ())), preferred_element_type=F32)
            s = jnp.where(valid, s + _alibi_slope(2 * p + e, n_heads) * ndist, NEG_INF)
            m = jnp.max(s, axis=-1, keepdims=True)
            pr = jnp.exp(s - m)
            l = jnp.sum(pr, axis=-1, keepdims=True)
            o_pair = o_pair + jnp.dot(pr.astype(BF16), ve, preferred_element_type=F32) * (1.0 / l)
            lse_pair = jnp.where(keep, m + jnp.log(l), lse_pair)
        o_ref[0, :, sl] = o_pair
        lse_ref[0, :, sl] = lse_pair


def _dilated_prompt(q, k, v, *, batch, dilation):
    t, width = q.shape
    s_len = t // batch
    d = dilation
    tq = SUB_WINDOW
    nb = s_len // (d * tq)
    assert nb * d * tq == s_len
    view = lambda a: a.reshape(batch, s_len // d, d * width)
    cur = lambda b, r, i: (b, i, r)
    prev = lambda b, r, i: (b, jnp.maximum(i - 1, 0), r)
    blk = (1, tq, width)
    o, lse = pl.pallas_call(
        functools.partial(_dilated_prompt_kernel, dilation=d, n_heads=width // HEAD_DIM),
        out_shape=[jax.ShapeDtypeStruct((batch, s_len // d, d * width), F32)] * 2,
        grid=(batch, d, nb),
        in_specs=[pl.BlockSpec(blk, cur), pl.BlockSpec(blk, cur), pl.BlockSpec(blk, prev),
                  pl.BlockSpec(blk, cur), pl.BlockSpec(blk, prev)],
        out_specs=[pl.BlockSpec(blk, cur)] * 2,
        compiler_params=_params("parallel", "parallel", "arbitrary"),
        name=f"dilated_prompt_d{d}",
    )(view(q), view(k), view(k), view(v), view(v))
    return o.reshape(t, width), lse.reshape(t, width)


def _even_out_kernel(x_ref, bg_ref, u_ref, *rest, seq_rows, sample):
    if sample:
        s1_ref, s2_ref = rest[:2]
        rest = rest[2:]
    else:
        halo_ref = rest[0]
        rest = rest[1:]
    if sample:
        o1_ref, cw_ref, w_ref, out_ref = rest
    else:
        o1_ref, o2_ref, o3_ref, l1_ref, l2_ref, l3_ref, cw_ref, w_ref, out_ref = rest
    tm, cw = u_ref.shape
    u = u_ref[...]
    r1 = pltpu.roll(u, 1, 0)
    r2 = pltpu.roll(u, 2, 0)
    if sample:
        t_in_seq = lax.broadcasted_iota(jnp.int32, (tm, cw), 0) % seq_rows
        u1 = jnp.where(t_in_seq >= 1, r1, s1_ref[...])
        u2 = jnp.where(t_in_seq >= 2, r2, s2_ref[...])
    else:
        first = (pl.program_id(0) % (seq_rows // tm)) == 0
        halo = jnp.where(first, 0.0, halo_ref[...])
        row8 = lax.broadcasted_iota(jnp.int32, halo.shape, 0)
        top1 = jnp.where(row8 < 1, pltpu.roll(halo, 1, 0), r1[0:8])
        top2 = jnp.where(row8 < 2, pltpu.roll(halo, 2, 0), r2[0:8])
        u1 = jnp.concatenate([top1, r1[8:]], axis=0)
        u2 = jnp.concatenate([top2, r2[8:]], axis=0)
    cwt = cw_ref[...]
    a = bg_ref[...] * (cwt[0:1] * u2 + cwt[1:2] * u1 + cwt[2:3] * u)
    if sample:
        b = o1_ref[...]
    else:
        l1, l2, l3 = l1_ref[...], l2_ref[...], l3_ref[...]
        mx = jnp.maximum(jnp.maximum(l1, l2), l3)
        e1, e2, e3 = jnp.exp(l1 - mx), jnp.exp(l2 - mx), jnp.exp(l3 - mx)
        b = (e1 * o1_ref[...] + e2 * o2_ref[...] + e3 * o3_ref[...]) * (1.0 / (e1 + e2 + e3))
    out_ref[...] = (x_ref[...]
                    + jnp.dot(a.astype(BF16), w_ref[0:cw, :], preferred_element_type=F32)
                    + jnp.dot(b.astype(BF16), w_ref[cw:, :], preferred_element_type=F32))


def _even_out(x, bg, u, hist, outs, lses, conv_w, w_out, *, tm, seq_rows, sample):
    t, d = x.shape
    cw = u.shape[1]
    row = lambda i: (i, 0)
    fix = lambda i: (0, 0)
    tile = pl.BlockSpec((tm, cw), row)
    if sample:
        hist_in, hist_specs = list(hist), [tile, tile]
    else:
        hist_in = [u]
        hist_specs = [pl.BlockSpec((8, cw), lambda i: (jnp.maximum(i * (tm // 8) - 1, 0), 0))]
    return pl.pallas_call(
        functools.partial(_even_out_kernel, seq_rows=seq_rows, sample=sample),
        out_shape=jax.ShapeDtypeStruct((t, d), F32),
        grid=(t // tm,),
        in_specs=[pl.BlockSpec((tm, d), row), tile, tile] + hist_specs + [tile] * (len(outs) + len(lses))
                 + [pl.BlockSpec(conv_w.shape, fix), pl.BlockSpec(w_out.shape, fix)],
        out_specs=pl.BlockSpec((tm, d), row),
        compiler_params=_params("parallel"),
        name="even_out",
    )(x, bg, u, *hist_in, *outs, *lses, conv_w, w_out)


def _diff_prompt_kernel(lam_ref, sub_ref, slope_ref, q_ref, k_ref, vt_ref, o_ref, qs_ref, acc_ref, *, lam_init):
    i = pl.program_id(2)
    tq = q_ref.shape[1]
    n_chunk = 2 * tq // LANES
    q = q_ref[0]
    lane_hi = lax.broadcasted_iota(jnp.int32, (1, LANES), 1) >= HEAD_DIM
    qs_ref[0:tq, :] = jnp.where(lane_hi, jnp.zeros_like(q), q)
    qs_ref[tq:, :] = jnp.where(lane_hi, q, jnp.zeros_like(q))
    acc_ref[...] = jnp.zeros_like(acc_ref)
    slope = slope_ref[0]
    key_row = lax.broadcasted_iota(jnp.int32, (tq, LANES), 0)
    q_col = lax.broadcasted_iota(jnp.int32, (tq, LANES), 1)
    key_row_f = key_row.astype(F32)

    def step(j, m, l, diagonal):
        k0 = pl.multiple_of(j * tq, tq)
        s = lax.dot_general(k_ref[0, pl.ds(k0, tq), :], qs_ref[...], (((1,), (1,)), ((), ())),
                            preferred_element_type=F32)
        bias = slope * (key_row_f + ((j - i) * tq).astype(F32))
        chunks = []
        for c in range(n_chunk):
            sc = s[:, c * LANES:(c + 1) * LANES] + bias
            if diagonal:
                sc = jnp.where(key_row <= q_col + (c * LANES) % tq, sc, NEG_INF)
            chunks.append(sc)
        s = jnp.concatenate(chunks, axis=1)
        m_new = jnp.maximum(m, jnp.max(s, axis=0, keepdims=True))
        alpha = jnp.exp2(m - m_new)
        p = jnp.exp2(s - m_new)
        l_new = alpha * l + jnp.sum(p, axis=0, keepdims=True)
        acc_ref[...] = alpha * acc_ref[...] + jnp.dot(vt_ref[j], p.astype(BF16), preferred_element_type=F32)
        return m_new, l_new

    m0 = jnp.full((1, 2 * tq), NEG_INF, F32)
    l0 = jnp.zeros((1, 2 * tq), F32)
    m, l = lax.fori_loop(0, i, lambda j, c: step(j, c[0], c[1], False), (m0, l0))
    m, l = step(i, m, l, True)

    lm = lam_ref[...]
    lam = (jnp.exp(jnp.sum(lm[0:1] * lm[1:2], axis=-1, keepdims=True))
           - jnp.exp(jnp.sum(lm[2:3] * lm[3:4], axis=-1, keepdims=True)) + lam_init)
    acc = acc_ref[...]
    inv = 1.0 / l
    o_t = acc[:, :tq] * inv[:, :tq] - lam * (acc[:, tq:] * inv[:, tq:])
    o = o_t.T
    o = o * _rms_rows(o) * sub_ref[...] * (1.0 - lam_init)
    o_ref[0] = o.astype(BF16)


def _diff_prompt(q, k, vt, lam, subln, *, batch, lam_init, tq):
    t, width = q.shape
    s_len = t // batch
    hw = 2 * HEAD_DIM
    n_heads = width // hw
    nq = s_len // tq
    assert vt.shape == (t // tq, width, tq) and nq * tq == s_len
    slopes = jnp.asarray([_alibi_slope(h, n_heads) * LOG2E for h in range(n_heads)], F32)
    slope_tab = jnp.broadcast_to(slopes[:, None, None], (n_heads, 1, LANES))
    out = pl.pallas_call(
        functools.partial(_diff_prompt_kernel, lam_init=lam_init),
        out_shape=jax.ShapeDtypeStruct((batch, s_len, width), BF16),
        grid=(batch, n_heads, nq),
        in_specs=[
            pl.BlockSpec(lam.shape, lambda b, h, i: (0, 0)),
            pl.BlockSpec((1, hw), lambda b, h, i: (0, 0)),
            pl.BlockSpec((1, 1, LANES), lambda b, h, i: (h, 0, 0)),
            pl.BlockSpec((1, tq, hw), lambda b, h, i: (b, i, h)),
            pl.BlockSpec((1, s_len, hw), lambda b, h, i: (b, 0, h)),
            pl.BlockSpec((nq, hw, tq), lambda b, h, i: (b, h, 0)),
        ],
        out_specs=pl.BlockSpec((1, tq, hw), lambda b, h, i: (b, i, h)),
        scratch_shapes=[pltpu.VMEM((2 * tq, hw), BF16), pltpu.VMEM((hw, 2 * tq), F32)],
        compiler_params=_params("parallel", "parallel", "arbitrary"),
        name="diff_prompt",
    )(lam, subln.reshape(1, hw), slope_tab, q.reshape(batch, s_len, width), k.reshape(batch, s_len, width), vt)
    return out.reshape(t, width)


def _nt(a, b):
    return lax.dot_general(a, b, (((1,), (1,)), ((), ())), preferred_element_type=F32)


def _dilated_sample_tables(n_tok, n_heads, n_pos):
    r = jnp.arange(n_tok * n_heads)[:, None]
    t, h = r // n_heads, r % n_heads
    slope = jnp.exp2(-8.0 * (h + 1) / n_heads).astype(F32)
    c = jnp.arange(n_pos * n_heads)[None, :]
    grp = n_tok * n_heads
    own = (c % grp) == r
    step = (SUB_WINDOW - c // grp).astype(F32)
    tabs = [jnp.where(own, -slope * (d * step), NEG_INF) for d in DILATIONS[1:]]
    pos = c // n_heads
    dist1 = (n_pos + t - pos).astype(F32)
    tab1 = jnp.where((c % n_heads == h) & (dist1 <= SUB_WINDOW), -slope * dist1, NEG_INF)
    cn = jnp.arange(LANES)[None, :]
    j = cn // n_heads
    tabn = jnp.where((cn < grp) & (cn % n_heads == h) & (j <= t),
                     -slope * (t - j).astype(F32) + jnp.where(j == t, math.log(len(DILATIONS)), 0.0), NEG_INF)
    return tabs[0], tab1, tabs[1], tabn


def _dilated_sample_kernel(b4_ref, b1_ref, b16_ref, bn_ref, q_ref, kn_ref, vn_ref,
                           k4_ref, k16_ref, v4_ref, v16_ref, o_ref):
    n_tok, n_heads, hd = q_ref.shape[1:]
    rows = n_tok * n_heads
    flat = lambda ref: ref[...].reshape(-1, hd)
    w = flat(q_ref).astype(BF16)
    pad = jnp.zeros((LANES - rows, hd), F32)
    kn = jnp.concatenate([flat(kn_ref), pad], axis=0).astype(BF16)
    vn = jnp.concatenate([flat(vn_ref), pad], axis=0).astype(BF16)
    s4 = _nt(w, flat(k4_ref).astype(BF16))
    a4 = s4 + b4_ref[...]
    a1 = s4 + b1_ref[...]
    a16 = _nt(w, flat(k16_ref).astype(BF16)) + b16_ref[...]
    an = _nt(w, kn) + bn_ref[...]
    rmax = lambda a: jnp.max(a, axis=-1, keepdims=True)
    m = jnp.maximum(jnp.maximum(rmax(a4), rmax(a1)), jnp.maximum(rmax(a16), rmax(an)))
    e4 = jnp.exp(a4 - m) + jnp.exp(a1 - m)
    e16 = jnp.exp(a16 - m)
    en = jnp.exp(an - m)
    rsum = lambda a: jnp.sum(a, axis=-1, keepdims=True)
    l = rsum(e4) + rsum(e16) + rsum(en)
    o = (jnp.dot(e4.astype(BF16), flat(v4_ref).astype(BF16), preferred_element_type=F32)
         + jnp.dot(e16.astype(BF16), flat(v16_ref).astype(BF16), preferred_element_type=F32)
         + jnp.dot(en.astype(BF16), vn, preferred_element_type=F32))
    o_ref[0] = (o * (1.0 / l)).reshape(n_tok, n_heads, hd)


def _dilated_sample(q, k_new, v_new, cache_k, cache_v, layer_j):
    n, n_tok, n_heads, hd = q.shape
    n_layers, _, w_buf = cache_k.shape[:3]
    d4, d16 = DILATIONS[1:]
    assert DILATIONS[0] == 1 and n_tok == d4 and w_buf == d16 * SUB_WINDOW
    n_pos = d4 * SUB_WINDOW
    tabs = _dilated_sample_tables(n_tok, n_heads, n_pos)
    fix = lambda i: (0, 0)
    tok = pl.BlockSpec((1, n_tok, n_heads, hd), lambda i: (i, 0, 0, 0))
    blk4 = pl.BlockSpec((1, 1, n_pos, n_heads, hd), lambda i: (layer_j, i, w_buf // n_pos - 1, 0, 0))
    blk16 = pl.BlockSpec((1, 1, SUB_WINDOW, n_tok, n_heads, hd), lambda i: (layer_j, i, 0, 0, 0, 0))
    by16 = lambda a: a.reshape(n_layers, n, SUB_WINDOW, d16, n_heads, hd)
    return pl.pallas_call(
        _dilated_sample_kernel,
        out_shape=jax.ShapeDtypeStruct((n, n_tok, n_heads, hd), F32),
        grid=(n,),
        in_specs=[pl.BlockSpec(tb.shape, fix) for tb in tabs] + [tok, tok, tok, blk4, blk16, blk4, blk16],
        out_specs=tok,
        compiler_params=_params("parallel"),
        name="dilated_sample",
    )(*tabs, q, k_new, v_new, cache_k, by16(cache_k), cache_v, by16(cache_v))


def _diff_sample_tables(n_tok, n_heads):
    r = jnp.arange(2 * n_tok * n_heads)[:, None]
    t, h = (r // n_heads) % n_tok, r % n_heads
    slope = (jnp.exp2(-8.0 * (h + 1) / n_heads) * LOG2E).astype(F32)
    c = jnp.arange(PAGE_SIZE * n_heads)[None, :]
    tab_page = jnp.where(c % n_heads == h, slope * (c // n_heads - PAGE_SIZE).astype(F32), NEG_INF)
    cn = jnp.arange(LANES)[None, :]
    j = cn // n_heads
    tab_new = jnp.where((cn < n_tok * n_heads) & (cn % n_heads == h) & (j <= t), slope * j.astype(F32), NEG_INF)
    return tab_page, tab_new, slope


def _diff_sample_kernel(pt_ref, lam_ref, sub_ref, bp_ref, bn_ref, slope_ref, q_ref, kn_ref, vn_ref, *rest,
                        lam_init, group, n_pages):
    k_refs, v_refs = rest[:group], rest[group:2 * group]
    o_ref, w_ref, m_ref, l_ref, acc_ref = rest[2 * group:]
    g = pl.program_id(1)
    n_tok, n_heads, hw = q_ref.shape[1:]
    rows = n_tok * n_heads
    flat = lambda ref: ref[...].reshape(-1, hw)

    @pl.when(g == 0)
    def _():
        q = flat(q_ref)
        lane_hi = lax.broadcasted_iota(jnp.int32, (1, hw), 1) >= HEAD_DIM
        w_ref[0:rows, :] = jnp.where(lane_hi, 0.0, q).astype(BF16)
        w_ref[rows:, :] = jnp.where(lane_hi, q, 0.0).astype(BF16)
        m_ref[...] = jnp.full_like(m_ref, NEG_INF)
        l_ref[...] = jnp.zeros_like(l_ref)
        acc_ref[...] = jnp.zeros_like(acc_ref)

    def absorb(s, offset, v):
        m_prev = m_ref[...]
        m_new = jnp.maximum(m_prev, jnp.max(s, axis=-1, keepdims=True) + offset)
        alpha = jnp.exp2(m_prev - m_new)
        p = jnp.exp2(s - (m_new - offset))
        l_ref[...] = alpha * l_ref[...] + jnp.sum(p, axis=-1, keepdims=True)
        acc_ref[...] = alpha * acc_ref[...] + jnp.dot(p.astype(BF16), v, preferred_element_type=F32)
        m_ref[...] = m_new

    slope = slope_ref[...]
    for i in range(group):
        page_end = ((g * group + i + 1 - n_pages) * PAGE_SIZE).astype(F32)
        s = _nt(w_ref[...], flat(k_refs[i]).astype(BF16)) + bp_ref[...]
        absorb(s, slope * page_end, flat(v_refs[i]).astype(BF16))

    @pl.when(g == pl.num_programs(1) - 1)
    def _():
        pad = jnp.zeros((LANES - rows, hw), F32)
        kn = jnp.concatenate([flat(kn_ref), pad], axis=0).astype(BF16)
        vn = jnp.concatenate([flat(vn_ref), pad], axis=0).astype(BF16)
        absorb(_nt(w_ref[...], kn) + bn_ref[...], 0.0, vn)
        lm = lam_ref[...]
        lam = (jnp.exp(jnp.sum(lm[0:1] * lm[1:2], axis=-1, keepdims=True))
               - jnp.exp(jnp.sum(lm[2:3] * lm[3:4], axis=-1, keepdims=True)) + lam_init)
        o = acc_ref[...] * (1.0 / l_ref[...])
        o = o[0:rows] - lam * o[rows:]
        o = o * _rms_rows(o) * sub_ref[...] * (1.0 - lam_init)
        o_ref[0] = o.reshape(n_tok, n_heads, hw)


def _diff_sample(q, k_new, v_new, cache_k, cache_v, page_table, lam, subln, layer_j, *, lam_init, group):
    n, n_tok, n_heads, hw = q.shape
    n_pages = page_table.shape[1]
    assert n_pages % group == 0
    rows = 2 * n_tok * n_heads
    tab_page, tab_new, slope = _diff_sample_tables(n_tok, n_heads)
    fix = lambda i, g, pt: (0, 0)
    tok = pl.BlockSpec((1, n_tok, n_heads, hw), lambda i, g, pt: (i, 0, 0, 0))

    def page_spec(slot):
        return pl.BlockSpec((1, 1, PAGE_SIZE, n_heads, hw),
                            lambda i, g, pt: (layer_j, pt[i, g * group + slot], 0, 0, 0))

    pages = [page_spec(s) for s in range(group)]
    return pl.pallas_call(
        functools.partial(_diff_sample_kernel, lam_init=lam_init, group=group, n_pages=n_pages),
        out_shape=jax.ShapeDtypeStruct((n, n_tok, n_heads, hw), F32),
        grid_spec=pltpu.PrefetchScalarGridSpec(
            num_scalar_prefetch=1,
            grid=(n, n_pages // group),
            in_specs=[pl.BlockSpec(lam.shape, fix), pl.BlockSpec((1, hw), fix), pl.BlockSpec(tab_page.shape, fix),
                      pl.BlockSpec(tab_new.shape, fix), pl.BlockSpec(slope.shape, fix), tok, tok, tok]
                     + pages + pages,
            out_specs=tok,
            scratch_shapes=[pltpu.VMEM((rows, hw), BF16), pltpu.VMEM((rows, 1), F32),
                            pltpu.VMEM((rows, 1), F32), pltpu.VMEM((rows, hw), F32)],
        ),
        compiler_params=_params("parallel", "arbitrary"),
        name="diff_sample",
    )(page_table, lam, subln.reshape(1, hw), tab_page, tab_new, slope, q, k_new, v_new,
      *([cache_k] * group), *([cache_v] * group))


def _row_tile(rows, target):
    tile = min(rows, target)
    while rows % tile:
        tile //= 2
    return tile


def kernel(x_prompt, x_sample, state_conv, cache_b_k, cache_b_v, cache_c_k, cache_c_v, page_table, norm_gain,
           ffn_w_gate, ffn_w_up, ffn_w_down, w_in_even, conv_w, qk_gain_b, w_out_even, w_in_odd, qk_gain_c,
           lambda_c, subln_c, w_out_odd):
    batch, seq, d = x_prompt.shape
    n, n_tok, _ = x_sample.shape
    depth = norm_gain.shape[0]
    w_buf = cache_b_k.shape[2]
    xp = x_prompt.reshape(batch * seq, d)
    xs = x_sample.reshape(n * n_tok, d)
    tp_ffn, tp, ts = _row_tile(batch * seq, 1024), _row_tile(batch * seq, 512), _row_tile(n * n_tok, 512)
    tq = _row_tile(seq, 256)
    wg, wu, wd = ffn_w_gate.astype(BF16), ffn_w_up.astype(BF16), ffn_w_down.astype(BF16)
    conv_p, conv_s, bk_p, bv_p, bk_s, bv_s, ck_p, cv_p, ck_s, cv_s = ([] for _ in range(10))

    for layer in range(depth):
        g = norm_gain[layer]
        j = layer // 2
        xp = _half_ffn(xp, g[0], wg[layer, 0], wu[layer, 0], wd[layer, 0], tm=tp_ffn)
        xs = _half_ffn(xs, g[0], wg[layer, 0], wu[layer, 0], wd[layer, 0], tm=ts)
        if layer % 2 == 0:
            w_in, w_out = w_in_even[j].astype(BF16), w_out_even[j].astype(BF16)
            cw = w_in.shape[1] // 6
            heads = (cw // HEAD_DIM, HEAD_DIM)
            bg, u, qb, kb, vb, kf, vf = _inproj_even(xp, g[1], w_in, qk_gain_b[j], tm=tp)
            res = [_dilated_prompt(qb, kb, vb, batch=batch, dilation=dil) for dil in DILATIONS]
            xp = _even_out(xp, bg, u, None, [r[0] for r in res], [r[1] for r in res], conv_w[j], w_out,
                           tm=tp, seq_rows=seq, sample=False)
            conv_p.append(u.reshape(batch, seq, cw)[:, seq - (CONV_K - 1):])
            bk_p.append(kf.reshape(batch, seq, *heads)[:, seq - w_buf:])
            bv_p.append(vf.reshape(batch, seq, *heads)[:, seq - w_buf:])

            bg, u, qb, kb, vb, kf, vf = _inproj_even(xs, g[1], w_in, qk_gain_b[j], tm=ts)
            shp = (n, n_tok) + heads
            o = _dilated_sample(qb.astype(F32).reshape(shp), kf.reshape(shp), vf.reshape(shp),
                                cache_b_k, cache_b_v, j)
            st = state_conv[j]
            hist2 = jnp.pad(st, ((0, 0), (0, n_tok - (CONV_K - 1)), (0, 0))).reshape(n * n_tok, cw)
            hist1 = jnp.pad(st[:, 1:], ((0, 0), (0, n_tok - 1), (0, 0))).reshape(n * n_tok, cw)
            xs = _even_out(xs, bg, u, (hist1, hist2), [o.reshape(n * n_tok, cw)], [], conv_w[j], w_out,
                           tm=ts, seq_rows=n_tok, sample=True)
            conv_s.append(u.reshape(n, n_tok, cw)[:, n_tok - (CONV_K - 1):])
            bk_s.append(kf.reshape(shp))
            bv_s.append(vf.reshape(shp))
        else:
            w_in, w_out = w_in_odd[j].astype(BF16), w_out_odd[j].astype(BF16)
            cw = w_in.shape[1] // 3
            heads = (cw // (2 * HEAD_DIM), 2 * HEAD_DIM)
            lam_init = 0.8 - 0.6 * math.exp(-0.3 * layer)
            qb, kb, kf, vf, vt = _inproj_odd(xp, g[1], w_in, qk_gain_c[j], tm=tp, kv_block=tq)
            o = _diff_prompt(qb, kb, vt, lambda_c[j], subln_c[j], batch=batch, lam_init=lam_init, tq=tq)
            xp = _outproj(xp, o, w_out, tm=tp)
            ck_p.append(kf.reshape(batch, seq // PAGE_SIZE, PAGE_SIZE, *heads))
            cv_p.append(vf.reshape(batch, seq // PAGE_SIZE, PAGE_SIZE, *heads))

            qb, kb, kf, vf = _inproj_odd(xs, g[1], w_in, qk_gain_c[j], tm=ts)
            shp = (n, n_tok) + heads
            o = _diff_sample(qb.astype(F32).reshape(shp), kf.reshape(shp), vf.reshape(shp), cache_c_k, cache_c_v,
                             page_table, lambda_c[j], subln_c[j], j, lam_init=lam_init, group=4)
            xs = _outproj(xs, o.reshape(n * n_tok, cw), w_out, tm=ts)
            ck_s.append(kf.reshape(shp))
            cv_s.append(vf.reshape(shp))
        xp = _half_ffn(xp, g[2], wg[layer, 1], wu[layer, 1], wd[layer, 1], tm=tp_ffn)
        xs = _half_ffn(xs, g[2], wg[layer, 1], wu[layer, 1], wd[layer, 1], tm=ts)

    return (xp.reshape(batch, seq, d), xs.reshape(n, n_tok, d),
            jnp.stack(conv_p), jnp.stack(conv_s),
            jnp.stack(bk_p), jnp.stack(bv_p), jnp.stack(bk_s), jnp.stack(bv_s),
            jnp.stack(ck_p), jnp.stack(cv_p), jnp.stack(ck_s), jnp.stack(cv_s))
```

```python
import functools
import math

import jax
import jax.numpy as jnp
from jax import lax
from jax.experimental import pallas as pl
from jax.experimental.pallas import tpu as pltpu

F32 = jnp.float32
BF16 = jnp.bfloat16

HEAD_DIM = 64
CONV_K = 3
SUB_WINDOW = 128
DILATIONS = (1, 4, 16)
PAGE_SIZE = 128
ATTN_SCALE = HEAD_DIM ** -0.5
LOG2E = math.log2(math.e)
RMS_EPS = 1e-6
NEG_INF = -1e30

V7X_VMEM_LIMIT_BYTES = 56 * 1024 * 1024
MXU_COLS = 256
LANES = 128


def _params(*sem):
    return pltpu.CompilerParams(dimension_semantics=sem, vmem_limit_bytes=V7X_VMEM_LIMIT_BYTES)


def _rms_rows(x):
    return lax.rsqrt(jnp.mean(x * x, axis=-1, keepdims=True) + RMS_EPS)


def _nt(a, b):
    return lax.dot_general(a, b, (((1,), (1,)), ((), ())), preferred_element_type=F32)


def _group_ones(width, group):
    i = jnp.arange(width) // group
    return (i[:, None] == i[None, :]).astype(BF16)


def _group_rmsnorm(x, bd_ref, gain):
    x2 = x * x
    hi = x2.astype(BF16)
    lo = (x2 - hi.astype(F32)).astype(BF16)
    bd = bd_ref[...]
    cols = []
    for c in range(x.shape[1] // MXU_COLS):
        sl = slice(c * MXU_COLS, (c + 1) * MXU_COLS)
        cols.append(jnp.dot(hi[:, sl], bd, preferred_element_type=F32)
                    + jnp.dot(lo[:, sl], bd, preferred_element_type=F32))
    ss = jnp.concatenate(cols, axis=1)
    return x * lax.rsqrt(ss * (1.0 / HEAD_DIM) + RMS_EPS) * gain


def _ffn_kernel(x_ref, g_ref, wg_ref, wu_ref, wd_ref, o_ref, h_ref, acc_ref):
    j = pl.program_id(1)

    @pl.when(j == 0)
    def _():
        x = x_ref[...]
        h_ref[...] = (x * _rms_rows(x) * g_ref[...]).astype(BF16)
        acc_ref[...] = jnp.zeros_like(acc_ref)

    h = h_ref[...]
    a = jnp.dot(h, wg_ref[...], preferred_element_type=F32)
    b = jnp.dot(h, wu_ref[...], preferred_element_type=F32)
    z = (a * (1.0 / (1.0 + jnp.exp(-a))) * b).astype(BF16)
    acc_ref[...] += jnp.dot(z, wd_ref[...], preferred_element_type=F32)

    @pl.when(j == pl.num_programs(1) - 1)
    def _():
        o_ref[...] = x_ref[...] + 0.5 * acc_ref[...]


def _half_ffn(x, g, wg, wu, wd, *, tm, tf=MXU_COLS):
    t, d = x.shape
    f = wg.shape[1]
    return pl.pallas_call(
        _ffn_kernel,
        out_shape=jax.ShapeDtypeStruct((t, d), F32),
        grid=(t // tm, f // tf),
        in_specs=[
            pl.BlockSpec((tm, d), lambda i, j: (i, 0)),
            pl.BlockSpec((1, d), lambda i, j: (0, 0)),
            pl.BlockSpec((d, tf), lambda i, j: (0, j)),
            pl.BlockSpec((d, tf), lambda i, j: (0, j)),
            pl.BlockSpec((tf, d), lambda i, j: (j, 0)),
        ],
        out_specs=pl.BlockSpec((tm, d), lambda i, j: (i, 0)),
        scratch_shapes=[pltpu.VMEM((tm, d), BF16), pltpu.VMEM((tm, d), F32)],
        compiler_params=_params("parallel", "arbitrary"),
        name="half_ffn",
    )(x, g.reshape(1, d), wg, wu, wd)


def _inproj_even_kernel(x_ref, g_ref, w_ref, bd_ref, gq_ref, gk_ref,
                        bg_ref, u_ref, qb_ref, kb_ref, vb_ref, kf_ref, vf_ref):
    x = x_ref[...]
    h = (x * _rms_rows(x) * g_ref[...]).astype(BF16)
    cw = u_ref.shape[1]

    def col(c):
        return jnp.dot(h, w_ref[:, c * cw:(c + 1) * cw], preferred_element_type=F32)

    bg_ref[...] = col(0)
    u_ref[...] = col(1) * col(2)
    q = _group_rmsnorm(col(3), bd_ref, gq_ref[...])
    k = _group_rmsnorm(col(4), bd_ref, gk_ref[...])
    v = col(5)
    qb_ref[...] = (q * ATTN_SCALE).astype(BF16)
    kb_ref[...] = k.astype(BF16)
    vb_ref[...] = v.astype(BF16)
    kf_ref[...] = k
    vf_ref[...] = v


def _inproj_even(x, g, w, qk_g, *, tm):
    t, d = x.shape
    cw = w.shape[1] // 6
    reps = cw // HEAD_DIM
    row = lambda i: (i, 0)
    fix = lambda i: (0, 0)
    outs = [jax.ShapeDtypeStruct((t, cw), F32)] * 2 + [jax.ShapeDtypeStruct((t, cw), BF16)] * 3 \
        + [jax.ShapeDtypeStruct((t, cw), F32)] * 2
    return pl.pallas_call(
        _inproj_even_kernel,
        out_shape=outs,
        grid=(t // tm,),
        in_specs=[
            pl.BlockSpec((tm, d), row),
            pl.BlockSpec((1, d), fix),
            pl.BlockSpec(w.shape, fix),
            pl.BlockSpec((MXU_COLS, MXU_COLS), fix),
            pl.BlockSpec((1, cw), fix),
            pl.BlockSpec((1, cw), fix),
        ],
        out_specs=[pl.BlockSpec((tm, cw), row)] * 7,
        compiler_params=_params("parallel"),
        name="inproj_even",
    )(x, g.reshape(1, d), w, _group_ones(MXU_COLS, HEAD_DIM),
      jnp.tile(qk_g[0], reps).reshape(1, cw), jnp.tile(qk_g[1], reps).reshape(1, cw))


def _inproj_odd_kernel(x_ref, g_ref, w_ref, bd_ref, gq_ref, gk_ref, *rest, kv_block):
    if kv_block:
        wvt_ref, qb_ref, kb_ref, kf_ref, vf_ref, vt_ref = rest
    else:
        qb_ref, kb_ref, kf_ref, vf_ref = rest
    x = x_ref[...]
    h = (x * _rms_rows(x) * g_ref[...]).astype(BF16)
    cw = qb_ref.shape[1]

    def col(c):
        return jnp.dot(h, w_ref[:, c * cw:(c + 1) * cw], preferred_element_type=F32)

    q = _group_rmsnorm(col(0), bd_ref, gq_ref[...])
    k = _group_rmsnorm(col(1), bd_ref, gk_ref[...])
    qb_ref[...] = (q * (ATTN_SCALE * LOG2E)).astype(BF16)
    kb_ref[...] = k.astype(BF16)
    kf_ref[...] = k
    vf_ref[...] = col(2)
    if kv_block:
        vt = lax.dot_general(wvt_ref[...], h, (((1,), (1,)), ((), ())), preferred_element_type=F32).astype(BF16)
        for c in range(vt_ref.shape[0]):
            vt_ref[c] = vt[:, c * kv_block:(c + 1) * kv_block]


def _inproj_odd(x, g, w, qk_g, *, tm, kv_block=0):
    t, d = x.shape
    cw = w.shape[1] // 3
    reps = cw // HEAD_DIM
    row = lambda i: (i, 0)
    fix = lambda i: (0, 0)
    ins = [x, g.reshape(1, d), w, _group_ones(MXU_COLS, HEAD_DIM),
           jnp.tile(qk_g[0], reps).reshape(1, cw), jnp.tile(qk_g[1], reps).reshape(1, cw)]
    in_specs = [pl.BlockSpec((tm, d), row), pl.BlockSpec((1, d), fix), pl.BlockSpec(w.shape, fix),
                pl.BlockSpec((MXU_COLS, MXU_COLS), fix), pl.BlockSpec((1, cw), fix), pl.BlockSpec((1, cw), fix)]
    outs = [jax.ShapeDtypeStruct((t, cw), BF16)] * 2 + [jax.ShapeDtypeStruct((t, cw), F32)] * 2
    out_specs = [pl.BlockSpec((tm, cw), row)] * 4
    if kv_block:
        ins.append(w[:, 2 * cw:].T)
        in_specs.append(pl.BlockSpec((cw, d), fix))
        outs.append(jax.ShapeDtypeStruct((t // kv_block, cw, kv_block), BF16))
        out_specs.append(pl.BlockSpec((tm // kv_block, cw, kv_block), lambda i: (i, 0, 0)))
    return pl.pallas_call(
        functools.partial(_inproj_odd_kernel, kv_block=kv_block),
        out_shape=outs,
        grid=(t // tm,),
        in_specs=in_specs,
        out_specs=out_specs,
        compiler_params=_params("parallel"),
        name="inproj_odd",
    )(*ins)


def _outproj_kernel(x_ref, a_ref, w_ref, o_ref):
    o_ref[...] = x_ref[...] + jnp.dot(a_ref[...].astype(BF16), w_ref[...], preferred_element_type=F32)


def _outproj(x, a, w, *, tm):
    t, d = x.shape
    row = lambda i: (i, 0)
    return pl.pallas_call(
        _outproj_kernel,
        out_shape=jax.ShapeDtypeStruct((t, d), F32),
        grid=(t // tm,),
        in_specs=[pl.BlockSpec((tm, d), row), pl.BlockSpec((tm, a.shape[1]), row),
                  pl.BlockSpec(w.shape, lambda i: (0, 0))],
        out_specs=pl.BlockSpec((tm, d), row),
        compiler_params=_params("parallel"),
        name="outproj",
    )(x, a, w)


def _alibi_slope(h, n_heads):
    return 2.0 ** (-8.0 * (h + 1) / n_heads)


def _dilated_prompt_kernel(q_ref, kc_ref, kp_ref, vc_ref, vp_ref, o_ref, lse_ref, *, dilation, n_heads):
    blk = pl.program_id(2)
    tq = q_ref.shape[1]
    qi = lax.broadcasted_iota(jnp.int32, (tq, 2 * tq), 0)
    kj = lax.broadcasted_iota(jnp.int32, (tq, 2 * tq), 1)
    dist = qi + tq - kj
    valid = (dist >= 0) & (dist <= SUB_WINDOW) & ((kj >= tq) | (blk > 0))
    ndist = (-dilation * dist).astype(F32)
    lane_hi = lax.broadcasted_iota(jnp.int32, (1, LANES), 1) >= HEAD_DIM
    for p in range(n_heads // 2):
        sl = slice(p * LANES, (p + 1) * LANES)
        q = q_ref[0, :, sl]
        k = jnp.concatenate([kp_ref[0, :, sl], kc_ref[0, :, sl]], axis=0)
        v = jnp.concatenate([vp_ref[0, :, sl], vc_ref[0, :, sl]], axis=0)
        o_pair = jnp.zeros((tq, LANES), F32)
        lse_pair = jnp.zeros((tq, LANES), F32)
        for e in range(2):
            keep = lane_hi if e else jnp.logical_not(lane_hi)
            qe = jnp.where(keep, q, jnp.zeros_like(q))
            ve = jnp.where(keep, v, jnp.zeros_like(v))
            s = lax.dot_general(qe, k, (((1,), (1,)), ((), ())), preferred_element_type=F32)
            s = jnp.where(valid, s + _alibi_slope(2 * p + e, n_heads) * ndist, NEG_INF)
            m = jnp.max(s, axis=-1, keepdims=True)
            pr = jnp.exp(s - m)
            l = jnp.sum(pr, axis=-1, keepdims=True)
            o_pair = o_pair + jnp.dot(pr.astype(BF16), ve, preferred_element_type=F32) * (1.0 / l)
            lse_pair = jnp.where(keep, m + jnp.log(l), lse_pair)
        o_ref[0, :, sl] = o_pair
        lse_ref[0, :, sl] = lse_pair


def _dilated_prompt(q, k, v, *, batch, dilation):
    t, width = q.shape
    s_len = t // batch
    d = dilation
    tq = SUB_WINDOW
    nb = s_len // (d * tq)
    assert nb * d * tq == s_len
    view = lambda a: a.reshape(batch, s_len // d, d * width)
    cur = lambda b, r, i: (b, i, r)
    prev = lambda b, r, i: (b, jnp.maximum(i - 1, 0), r)
    blk = (1, tq, width)
    o, lse = pl.pallas_call(
        functools.partial(_dilated_prompt_kernel, dilation=d, n_heads=width // HEAD_DIM),
        out_shape=[jax.ShapeDtypeStruct((batch, s_len // d, d * width), F32)] * 2,
        grid=(batch, d, nb),
        in_specs=[pl.BlockSpec(blk, cur), pl.BlockSpec(blk, cur), pl.BlockSpec(blk, prev),
                  pl.BlockSpec(blk, cur), pl.BlockSpec(blk, prev)],
        out_specs=[pl.BlockSpec(blk, cur)] * 2,
        compiler_params=_params("parallel", "parallel", "arbitrary"),
        name=f"dilated_prompt_d{d}",
    )(view(q), view(k), view(k), view(v), view(v))
    return o.reshape(t, width), lse.reshape(t, width)


def _even_out_kernel(x_ref, bg_ref, u_ref, *rest, seq_rows, sample):
    if sample:
        s1_ref, s2_ref = rest[:2]
        rest = rest[2:]
    else:
        halo_ref = rest[0]
        rest = rest[1:]
    if sample:
        o1_ref, cw_ref, w_ref, out_ref = rest
    else:
        o1_ref, o2_ref, o3_ref, l1_ref, l2_ref, l3_ref, cw_ref, w_ref, out_ref = rest
    tm, cw = u_ref.shape
    u = u_ref[...]
    r1 = pltpu.roll(u, 1, 0)
    r2 = pltpu.roll(u, 2, 0)
    if sample:
        t_in_seq = lax.broadcasted_iota(jnp.int32, (tm, cw), 0) % seq_rows
        u1 = jnp.where(t_in_seq >= 1, r1, s1_ref[...])
        u2 = jnp.where(t_in_seq >= 2, r2, s2_ref[...])
    else:
        first = (pl.program_id(0) % (seq_rows // tm)) == 0
        halo = jnp.where(first, 0.0, halo_ref[...])
        row8 = lax.broadcasted_iota(jnp.int32, halo.shape, 0)
        top1 = jnp.where(row8 < 1, pltpu.roll(halo, 1, 0), r1[0:8])
        top2 = jnp.where(row8 < 2, pltpu.roll(halo, 2, 0), r2[0:8])
        u1 = jnp.concatenate([top1, r1[8:]], axis=0)
        u2 = jnp.concatenate([top2, r2[8:]], axis=0)
    cwt = cw_ref[...]
    a = bg_ref[...] * (cwt[0:1] * u2 + cwt[1:2] * u1 + cwt[2:3] * u)
    if sample:
        b = o1_ref[...]
    else:
        l1, l2, l3 = l1_ref[...], l2_ref[...], l3_ref[...]
        mx = jnp.maximum(jnp.maximum(l1, l2), l3)
        e1, e2, e3 = jnp.exp(l1 - mx), jnp.exp(l2 - mx), jnp.exp(l3 - mx)
        b = (e1 * o1_ref[...] + e2 * o2_ref[...] + e3 * o3_ref[...]) * (1.0 / (e1 + e2 + e3))
    out_ref[...] = (x_ref[...]
                    + jnp.dot(a.astype(BF16), w_ref[0:cw, :], preferred_element_type=F32)
                    + jnp.dot(b.astype(BF16), w_ref[cw:, :], preferred_element_type=F32))


def _even_out(x, bg, u, hist, outs, lses, conv_w, w_out, *, tm, seq_rows, sample):
    t, d = x.shape
    cw = u.shape[1]
    row = lambda i: (i, 0)
    fix = lambda i: (0, 0)
    tile = pl.BlockSpec((tm, cw), row)
    if sample:
        hist_in, hist_specs = list(hist), [tile, tile]
    else:
        hist_in = [u]
        hist_specs = [pl.BlockSpec((8, cw), lambda i: (jnp.maximum(i * (tm // 8) - 1, 0), 0))]
    return pl.pallas_call(
        functools.partial(_even_out_kernel, seq_rows=seq_rows, sample=sample),
        out_shape=jax.ShapeDtypeStruct((t, d), F32),
        grid=(t // tm,),
        in_specs=[pl.BlockSpec((tm, d), row), tile, tile] + hist_specs + [tile] * (len(outs) + len(lses))
                 + [pl.BlockSpec(conv_w.shape, fix), pl.BlockSpec(w_out.shape, fix)],
        out_specs=pl.BlockSpec((tm, d), row),
        compiler_params=_params("parallel"),
        name="even_out",
    )(x, bg, u, *hist_in, *outs, *lses, conv_w, w_out)


def _diff_prompt_kernel(lam_ref, sub_ref, slope_ref, q_ref, k_ref, vt_ref, o_ref,
                        qs_ref, m_ref, l_ref, acc_ref, *, lam_init):
    i = pl.program_id(2)
    tq = q_ref.shape[1]
    q = q_ref[0]
    lane_hi = lax.broadcasted_iota(jnp.int32, (1, LANES), 1) >= HEAD_DIM
    qs_ref[0:tq, :] = jnp.where(lane_hi, jnp.zeros_like(q), q)
    qs_ref[tq:, :] = jnp.where(lane_hi, q, jnp.zeros_like(q))
    m_ref[...] = jnp.full_like(m_ref, NEG_INF)
    l_ref[...] = jnp.zeros_like(l_ref)
    acc_ref[...] = jnp.zeros_like(acc_ref)
    slope = slope_ref[0]
    key_row = lax.broadcasted_iota(jnp.int32, (tq, LANES), 0)
    q_col = lax.broadcasted_iota(jnp.int32, (tq, LANES), 1)
    key_row_f = key_row.astype(F32)

    def step(j, diagonal):
        kb = k_ref[0, pl.ds(pl.multiple_of(j * tq, tq), tq), :]
        vt = vt_ref[j]
        bias = slope * (key_row_f + ((j - i) * tq).astype(F32))
        m_all, l_all = m_ref[...], l_ref[...]
        n_grp = 2 * tq // MXU_COLS
        scores = [_nt(kb, qs_ref[c * MXU_COLS:(c + 1) * MXU_COLS, :]) for c in range(n_grp)]
        m_out, l_out = [], []
        for c in range(n_grp):
            cols = slice(c * MXU_COLS, (c + 1) * MXU_COLS)
            s = scores[c]
            parts = []
            for cc in range(MXU_COLS // LANES):
                sc = s[:, cc * LANES:(cc + 1) * LANES] + bias
                if diagonal:
                    sc = jnp.where(key_row <= q_col + (c * MXU_COLS + cc * LANES) % tq, sc, NEG_INF)
                parts.append(sc)
            s = jnp.concatenate(parts, axis=1)
            m_prev = m_all[:, cols]
            m_new = jnp.maximum(m_prev, jnp.max(s, axis=0, keepdims=True))
            alpha = jnp.exp2(m_prev - m_new)
            p = jnp.exp2(s - m_new)
            l_out.append(alpha * l_all[:, cols] + jnp.sum(p, axis=0, keepdims=True))
            acc_ref[:, cols] = alpha * acc_ref[:, cols] + jnp.dot(vt, p.astype(BF16), preferred_element_type=F32)
            m_out.append(m_new)
        m_ref[...] = jnp.concatenate(m_out, axis=1)
        l_ref[...] = jnp.concatenate(l_out, axis=1)

    def full_step(j, carry):
        step(j, False)
        return carry

    lax.fori_loop(0, i, full_step, 0)
    step(i, True)

    lm = lam_ref[...]
    lam = (jnp.exp(jnp.sum(lm[0:1] * lm[1:2], axis=-1, keepdims=True))
           - jnp.exp(jnp.sum(lm[2:3] * lm[3:4], axis=-1, keepdims=True)) + lam_init)
    acc = acc_ref[...]
    inv = 1.0 / l_ref[...]
    o_t = acc[:, :tq] * inv[:, :tq] - lam * (acc[:, tq:] * inv[:, tq:])
    o = o_t.T
    o = o * _rms_rows(o) * sub_ref[...] * (1.0 - lam_init)
    o_ref[0] = o.astype(BF16)


def _diff_prompt(q, k, vt, lam, subln, *, batch, lam_init, tq):
    t, width = q.shape
    s_len = t // batch
    hw = 2 * HEAD_DIM
    n_heads = width // hw
    nq = s_len // tq
    assert vt.shape == (t // tq, width, tq) and nq * tq == s_len
    slopes = jnp.asarray([_alibi_slope(h, n_heads) * LOG2E for h in range(n_heads)], F32)
    slope_tab = jnp.broadcast_to(slopes[:, None, None], (n_heads, 1, LANES))
    out = pl.pallas_call(
        functools.partial(_diff_prompt_kernel, lam_init=lam_init),
        out_shape=jax.ShapeDtypeStruct((batch, s_len, width), BF16),
        grid=(batch, n_heads, nq),
        in_specs=[
            pl.BlockSpec(lam.shape, lambda b, h, i: (0, 0)),
            pl.BlockSpec((1, hw), lambda b, h, i: (0, 0)),
            pl.BlockSpec((1, 1, LANES), lambda b, h, i: (h, 0, 0)),
            pl.BlockSpec((1, tq, hw), lambda b, h, i: (b, i, h)),
            pl.BlockSpec((1, s_len, hw), lambda b, h, i: (b, 0, h)),
            pl.BlockSpec((nq, hw, tq), lambda b, h, i: (b, h, 0)),
        ],
        out_specs=pl.BlockSpec((1, tq, hw), lambda b, h, i: (b, i, h)),
        scratch_shapes=[pltpu.VMEM((2 * tq, hw), BF16), pltpu.VMEM((1, 2 * tq), F32),
                        pltpu.VMEM((1, 2 * tq), F32), pltpu.VMEM((hw, 2 * tq), F32)],
        compiler_params=_params("parallel", "parallel", "arbitrary"),
        name="diff_prompt",
    )(lam, subln.reshape(1, hw), slope_tab, q.reshape(batch, s_len, width), k.reshape(batch, s_len, width), vt)
    return out.reshape(t, width)


SAMPLE_ROWS = 16


def _dilated_sample_tables(n_tok, w_buf):
    t = jnp.arange(SAMPLE_ROWS)[:, None]
    huge = -NEG_INF
    dist = (w_buf + t - jnp.arange(w_buf)[None, :])
    tabs = [jnp.where((t < n_tok) & (dist % d == 0) & (dist <= d * SUB_WINDOW), dist.astype(F32), huge)
            for d in DILATIONS]
    j = jnp.arange(LANES)[None, :]
    ok = (t < n_tok) & (j <= t)
    dist_new = jnp.where(ok, (t - j).astype(F32), huge)
    own = jnp.where(ok & (j == t), math.log(len(DILATIONS)), 0.0).astype(F32)
    return tabs + [dist_new, own]


def _dilated_sample_kernel(d1_ref, d4_ref, d16_ref, dn_ref, own_ref, q_ref, kn_ref, vn_ref, kt_ref, vt_ref, o_ref):
    n_heads, n_tok, hd = q_ref.shape[1:]
    pad_q = jnp.zeros((SAMPLE_ROWS - n_tok, hd), F32)
    pad_n = jnp.zeros((LANES - n_tok, hd), F32)
    rmax = lambda a: jnp.max(a, axis=-1, keepdims=True)
    rsum = lambda a: jnp.sum(a, axis=-1, keepdims=True)
    for h in range(n_heads):
        slope = _alibi_slope(h, n_heads)
        q = jnp.concatenate([q_ref[0, h], pad_q], axis=0).astype(BF16)
        kn = jnp.concatenate([kn_ref[0, h], pad_n], axis=0).astype(BF16)
        vn = jnp.concatenate([vn_ref[0, h], pad_n], axis=0).astype(BF16)
        s = jnp.dot(q, kt_ref[0, 0, h].astype(BF16), preferred_element_type=F32)
        a1 = s - slope * d1_ref[...]
        a4 = s - slope * d4_ref[...]
        a16 = s - slope * d16_ref[...]
        an = _nt(q, kn) - slope * dn_ref[...] + own_ref[...]
        m = jnp.maximum(jnp.maximum(rmax(a1), rmax(a4)), jnp.maximum(rmax(a16), rmax(an)))
        e = jnp.exp(a1 - m) + jnp.exp(a4 - m) + jnp.exp(a16 - m)
        en = jnp.exp(an - m)
        o = _nt(e.astype(BF16), vt_ref[0, 0, h].astype(BF16)) + jnp.dot(en.astype(BF16), vn,
                                                                        preferred_element_type=F32)
        o_ref[0, h] = (o * (1.0 / (rsum(e) + rsum(en))))[0:n_tok]


def _dilated_sample(q, k_new, v_new, cache_kt, cache_vt, layer_j, *, n_tok):
    n, n_heads, tok_pad, hd = q.shape
    w_buf = cache_kt.shape[-1]
    assert n_tok <= min(DILATIONS[1:]) and w_buf == max(DILATIONS) * SUB_WINDOW
    tabs = _dilated_sample_tables(n_tok, w_buf)
    fix = lambda i: (0, 0)
    tok = pl.BlockSpec((1, n_heads, tok_pad, hd), lambda i: (i, 0, 0, 0))
    blk = pl.BlockSpec((1, 1, n_heads, hd, w_buf), lambda i: (layer_j, i, 0, 0, 0))
    return pl.pallas_call(
        _dilated_sample_kernel,
        out_shape=jax.ShapeDtypeStruct((n, n_heads, tok_pad, hd), F32),
        grid=(n,),
        in_specs=[pl.BlockSpec(tb.shape, fix) for tb in tabs] + [tok, tok, tok, blk, blk],
        out_specs=tok,
        compiler_params=_params("parallel"),
        name="dilated_sample",
    )(*tabs, q, k_new, v_new, cache_kt, cache_vt)


def _diff_sample_tables(n_tok, n_heads):
    r = jnp.arange(2 * n_tok * n_heads)[:, None]
    t, h = (r // n_heads) % n_tok, r % n_heads
    slope = (jnp.exp2(-8.0 * (h + 1) / n_heads) * LOG2E).astype(F32)
    c = jnp.arange(PAGE_SIZE * n_heads)[None, :]
    tab_page = jnp.where(c % n_heads == h, slope * (c // n_heads - PAGE_SIZE).astype(F32), NEG_INF)
    cn = jnp.arange(LANES)[None, :]
    j = cn // n_heads
    tab_new = jnp.where((cn < n_tok * n_heads) & (cn % n_heads == h) & (j <= t), slope * j.astype(F32), NEG_INF)
    return tab_page, tab_new, slope


def _diff_sample_kernel(pt_ref, lam_ref, sub_ref, bp_ref, bn_ref, slope_ref, q_ref, kn_ref, vn_ref, *rest,
                        lam_init, group, n_pages):
    k_refs, v_refs = rest[:group], rest[group:2 * group]
    o_ref, w_ref, m_ref, l_ref, acc_ref = rest[2 * group:]
    g = pl.program_id(1)
    n_tok, n_heads, hw = q_ref.shape[1:]
    rows = n_tok * n_heads
    flat = lambda ref: ref[...].reshape(-1, hw)

    @pl.when(g == 0)
    def _():
        q = flat(q_ref)
        lane_hi = lax.broadcasted_iota(jnp.int32, (1, hw), 1) >= HEAD_DIM
        w_ref[0:rows, :] = jnp.where(lane_hi, 0.0, q).astype(BF16)
        w_ref[rows:, :] = jnp.where(lane_hi, q, 0.0).astype(BF16)
        m_ref[...] = jnp.full_like(m_ref, NEG_INF)
        l_ref[...] = jnp.zeros_like(l_ref)
        acc_ref[...] = jnp.zeros_like(acc_ref)

    def absorb(scores, offsets, values):
        m_prev = m_ref[...]
        m_new = m_prev
        for s, off in zip(scores, offsets):
            m_new = jnp.maximum(m_new, jnp.max(s, axis=-1, keepdims=True) + off)
        alpha = jnp.exp2(m_prev - m_new)
        l = alpha * l_ref[...]
        acc = alpha * acc_ref[...]
        for s, off, v in zip(scores, offsets, values):
            p = jnp.exp2(s - (m_new - off))
            l = l + jnp.sum(p, axis=-1, keepdims=True)
            acc = acc + jnp.dot(p.astype(BF16), v, preferred_element_type=F32)
        l_ref[...] = l
        acc_ref[...] = acc
        m_ref[...] = m_new

    slope = slope_ref[...]
    w = w_ref[...]
    absorb([_nt(w, flat(k_refs[i]).astype(BF16)) + bp_ref[...] for i in range(group)],
           [slope * ((g * group + i + 1 - n_pages) * PAGE_SIZE).astype(F32) for i in range(group)],
           [flat(v_refs[i]).astype(BF16) for i in range(group)])

    @pl.when(g == pl.num_programs(1) - 1)
    def _():
        pad = jnp.zeros((LANES - rows, hw), F32)
        kn = jnp.concatenate([flat(kn_ref), pad], axis=0).astype(BF16)
        vn = jnp.concatenate([flat(vn_ref), pad], axis=0).astype(BF16)
        absorb([_nt(w, kn) + bn_ref[...]], [0.0], [vn])
        lm = lam_ref[...]
        lam = (jnp.exp(jnp.sum(lm[0:1] * lm[1:2], axis=-1, keepdims=True))
               - jnp.exp(jnp.sum(lm[2:3] * lm[3:4], axis=-1, keepdims=True)) + lam_init)
        o = acc_ref[...] * (1.0 / l_ref[...])
        o = o[0:rows] - lam * o[rows:]
        o = o * _rms_rows(o) * sub_ref[...] * (1.0 - lam_init)
        o_ref[0] = o.reshape(n_tok, n_heads, hw)


def _diff_sample(q, k_new, v_new, cache_k, cache_v, page_table, lam, subln, layer_j, *, lam_init, group):
    n, n_tok, n_heads, hw = q.shape
    n_pages = page_table.shape[1]
    assert n_pages % group == 0
    rows = 2 * n_tok * n_heads
    tab_page, tab_new, slope = _diff_sample_tables(n_tok, n_heads)
    fix = lambda i, g, pt: (0, 0)
    tok = pl.BlockSpec((1, n_tok, n_heads, hw), lambda i, g, pt: (i, 0, 0, 0))

    def page_spec(slot):
        return pl.BlockSpec((1, 1, PAGE_SIZE, n_heads, hw),
                            lambda i, g, pt: (layer_j, pt[i, g * group + slot], 0, 0, 0))

    pages = [page_spec(s) for s in range(group)]
    return pl.pallas_call(
        functools.partial(_diff_sample_kernel, lam_init=lam_init, group=group, n_pages=n_pages),
        out_shape=jax.ShapeDtypeStruct((n, n_tok, n_heads, hw), F32),
        grid_spec=pltpu.PrefetchScalarGridSpec(
            num_scalar_prefetch=1,
            grid=(n, n_pages // group),
            in_specs=[pl.BlockSpec(lam.shape, fix), pl.BlockSpec((1, hw), fix), pl.BlockSpec(tab_page.shape, fix),
                      pl.BlockSpec(tab_new.shape, fix), pl.BlockSpec(slope.shape, fix), tok, tok, tok]
                     + pages + pages,
            out_specs=tok,
            scratch_shapes=[pltpu.VMEM((rows, hw), BF16), pltpu.VMEM((rows, 1), F32),
                            pltpu.VMEM((rows, 1), F32), pltpu.VMEM((rows, hw), F32)],
        ),
        compiler_params=_params("parallel", "arbitrary"),
        name="diff_sample",
    )(page_table, lam, subln.reshape(1, hw), tab_page, tab_new, slope, q, k_new, v_new,
      *([cache_k] * group), *([cache_v] * group))


def _row_tile(rows, target):
    tile = min(rows, target)
    while rows % tile:
        tile //= 2
    return tile


def kernel(x_prompt, x_sample, state_conv, cache_b_k, cache_b_v, cache_c_k, cache_c_v, page_table, norm_gain,
           ffn_w_gate, ffn_w_up, ffn_w_down, w_in_even, conv_w, qk_gain_b, w_out_even, w_in_odd, qk_gain_c,
           lambda_c, subln_c, w_out_odd):
    batch, seq, d = x_prompt.shape
    n, n_tok, _ = x_sample.shape
    depth = norm_gain.shape[0]
    w_buf = cache_b_k.shape[2]
    xp = x_prompt.reshape(batch * seq, d)
    xs = x_sample.reshape(n * n_tok, d)
    tp_ffn, tp, ts = _row_tile(batch * seq, 1024), _row_tile(batch * seq, 512), _row_tile(n * n_tok, 512)
    tq = _row_tile(seq, 512)
    wg, wu, wd = ffn_w_gate.astype(BF16), ffn_w_up.astype(BF16), ffn_w_down.astype(BF16)
    conv_p, conv_s, bk_p, bv_p, bk_s, bv_s, ck_p, cv_p, ck_s, cv_s = ([] for _ in range(10))

    for layer in range(depth):
        g = norm_gain[layer]
        j = layer // 2
        xp = _half_ffn(xp, g[0], wg[layer, 0], wu[layer, 0], wd[layer, 0], tm=tp_ffn)
        xs = _half_ffn(xs, g[0], wg[layer, 0], wu[layer, 0], wd[layer, 0], tm=ts)
        if layer % 2 == 0:
            w_in, w_out = w_in_even[j].astype(BF16), w_out_even[j].astype(BF16)
            cw = w_in.shape[1] // 6
            heads = (cw // HEAD_DIM, HEAD_DIM)
            bg, u, qb, kb, vb, kf, vf = _inproj_even(xp, g[1], w_in, qk_gain_b[j], tm=tp)
            res = [_dilated_prompt(qb, kb, vb, batch=batch, dilation=dil) for dil in DILATIONS]
            xp = _even_out(xp, bg, u, None, [r[0] for r in res], [r[1] for r in res], conv_w[j], w_out,
                           tm=tp, seq_rows=seq, sample=False)
            conv_p.append(u.reshape(batch, seq, cw)[:, seq - (CONV_K - 1):])
            bk_p.append(kf.reshape(batch, seq, *heads)[:, seq - w_buf:])
            bv_p.append(vf.reshape(batch, seq, *heads)[:, seq - w_buf:])

            bg, u, qb, kb, vb, kf, vf = _inproj_even(xs, g[1], w_in, qk_gain_b[j], tm=ts)
            shp = (n, n_tok) + heads
            tok_pad = -(-n_tok // 8) * 8

            def by_head(a):
                a = a.reshape(shp).transpose(0, 2, 1, 3)
                return jnp.pad(a, ((0, 0), (0, 0), (0, tok_pad - n_tok), (0, 0)))

            o = _dilated_sample(by_head(qb.astype(F32)), by_head(kf), by_head(vf),
                                cache_b_k.transpose(0, 1, 3, 4, 2), cache_b_v.transpose(0, 1, 3, 4, 2), j,
                                n_tok=n_tok)
            o = o[:, :, :n_tok].transpose(0, 2, 1, 3)
            st = state_conv[j]
            hist2 = jnp.pad(st, ((0, 0), (0, n_tok - (CONV_K - 1)), (0, 0))).reshape(n * n_tok, cw)
            hist1 = jnp.pad(st[:, 1:], ((0, 0), (0, n_tok - 1), (0, 0))).reshape(n * n_tok, cw)
            xs = _even_out(xs, bg, u, (hist1, hist2), [o.reshape(n * n_tok, cw)], [], conv_w[j], w_out,
                           tm=ts, seq_rows=n_tok, sample=True)
            conv_s.append(u.reshape(n, n_tok, cw)[:, n_tok - (CONV_K - 1):])
            bk_s.append(kf.reshape(shp))
            bv_s.append(vf.reshape(shp))
        else:
            w_in, w_out = w_in_odd[j].astype(BF16), w_out_odd[j].astype(BF16)
            cw = w_in.shape[1] // 3
            heads = (cw // (2 * HEAD_DIM), 2 * HEAD_DIM)
            lam_init = 0.8 - 0.6 * math.exp(-0.3 * layer)
            qb, kb, kf, vf, vt = _inproj_odd(xp, g[1], w_in, qk_gain_c[j], tm=tp, kv_block=tq)
            o = _diff_prompt(qb, kb, vt, lambda_c[j], subln_c[j], batch=batch, lam_init=lam_init, tq=tq)
            xp = _outproj(xp, o, w_out, tm=tp)
            ck_p.append(kf.reshape(batch, seq // PAGE_SIZE, PAGE_SIZE, *heads))
            cv_p.append(vf.reshape(batch, seq // PAGE_SIZE, PAGE_SIZE, *heads))

            qb, kb, kf, vf = _inproj_odd(xs, g[1], w_in, qk_gain_c[j], tm=ts)
            shp = (n, n_tok) + heads
            o = _diff_sample(qb.astype(F32).reshape(shp), kf.reshape(shp), vf.reshape(shp), cache_c_k, cache_c_v,
                             page_table, lambda_c[j], subln_c[j], j, lam_init=lam_init,
                             group=math.gcd(page_table.shape[1], 8))
            xs = _outproj(xs, o.reshape(n * n_tok, cw), w_out, tm=ts)
            ck_s.append(kf.reshape(shp))
            cv_s.append(vf.reshape(shp))
        xp = _half_ffn(xp, g[2], wg[layer, 1], wu[layer, 1], wd[layer, 1], tm=tp_ffn)
        xs = _half_ffn(xs, g[2], wg[layer, 1], wu[layer, 1], wd[layer, 1], tm=ts)

    return (xp.reshape(batch, seq, d), xs.reshape(n, n_tok, d),
            jnp.stack(conv_p), jnp.stack(conv_s),
            jnp.stack(bk_p), jnp.stack(bv_p), jnp.stack(bk_s), jnp.stack(bv_s),
            jnp.stack(ck_p), jnp.stack(cv_p), jnp.stack(ck_s), jnp.stack(cv_s))
```

```python
import functools
import math

import jax
import jax.numpy as jnp
from jax import lax
from jax.experimental import pallas as pl
from jax.experimental.pallas import tpu as pltpu

F32 = jnp.float32
BF16 = jnp.bfloat16

HEAD_DIM = 64
CONV_K = 3
SUB_WINDOW = 128
DILATIONS = (1, 4, 16)
PAGE_SIZE = 128
ATTN_SCALE = HEAD_DIM ** -0.5
LOG2E = math.log2(math.e)
RMS_EPS = 1e-6
NEG_INF = -1e30

V7X_VMEM_LIMIT_BYTES = 56 * 1024 * 1024
MXU_COLS = 256
LANES = 128


def _params(*sem):
    return pltpu.CompilerParams(dimension_semantics=sem, vmem_limit_bytes=V7X_VMEM_LIMIT_BYTES)


def _rms_rows(x):
    return lax.rsqrt(jnp.mean(x * x, axis=-1, keepdims=True) + RMS_EPS)


def _nt(a, b):
    return lax.dot_general(a, b, (((1,), (1,)), ((), ())), preferred_element_type=F32)


def _group_ones(width, group):
    i = jnp.arange(width) // group
    return (i[:, None] == i[None, :]).astype(BF16)


def _group_rmsnorm(x, bd_ref, gain):
    x2 = x * x
    hi = x2.astype(BF16)
    lo = (x2 - hi.astype(F32)).astype(BF16)
    bd = bd_ref[...]
    cols = []
    for c in range(x.shape[1] // MXU_COLS):
        sl = slice(c * MXU_COLS, (c + 1) * MXU_COLS)
        cols.append(jnp.dot(hi[:, sl], bd, preferred_element_type=F32)
                    + jnp.dot(lo[:, sl], bd, preferred_element_type=F32))
    ss = jnp.concatenate(cols, axis=1)
    return x * lax.rsqrt(ss * (1.0 / HEAD_DIM) + RMS_EPS) * gain


def _ffn_kernel(x_ref, g_ref, wg_ref, wu_ref, wd_ref, o_ref, h_ref, acc_ref):
    j = pl.program_id(1)

    @pl.when(j == 0)
    def _():
        x = x_ref[...]
        h_ref[...] = (x * _rms_rows(x) * g_ref[...]).astype(BF16)
        acc_ref[...] = jnp.zeros_like(acc_ref)

    h = h_ref[...]
    a = jnp.dot(h, wg_ref[...], preferred_element_type=F32)
    b = jnp.dot(h, wu_ref[...], preferred_element_type=F32)
    z = (a * (1.0 / (1.0 + jnp.exp(-a))) * b).astype(BF16)
    acc_ref[...] += jnp.dot(z, wd_ref[...], preferred_element_type=F32)

    @pl.when(j == pl.num_programs(1) - 1)
    def _():
        o_ref[...] = x_ref[...] + 0.5 * acc_ref[...]


def _half_ffn(x, g, wg, wu, wd, which, *, tm, tf=MXU_COLS):
    t, d = x.shape
    f = wg.shape[-1]
    return pl.pallas_call(
        _ffn_kernel,
        out_shape=jax.ShapeDtypeStruct((t, d), F32),
        grid=(t // tm, f // tf),
        in_specs=[
            pl.BlockSpec((tm, d), lambda i, j: (i, 0)),
            pl.BlockSpec((1, d), lambda i, j: (0, 0)),
            pl.BlockSpec((None, None, d, tf), lambda i, j: (*which, 0, j)),
            pl.BlockSpec((None, None, d, tf), lambda i, j: (*which, 0, j)),
            pl.BlockSpec((None, None, tf, d), lambda i, j: (*which, j, 0)),
        ],
        out_specs=pl.BlockSpec((tm, d), lambda i, j: (i, 0)),
        scratch_shapes=[pltpu.VMEM((tm, d), BF16), pltpu.VMEM((tm, d), F32)],
        compiler_params=_params("parallel", "arbitrary"),
        name="half_ffn",
    )(x, g.reshape(1, d), wg, wu, wd)


def _inproj_even_kernel(x_ref, g_ref, w_ref, bd_ref, gq_ref, gk_ref, bg_ref, u_ref, kf_ref, vf_ref, *rest, dilations):
    n_d = len(dilations)
    stages = rest[3 * n_d:]
    x = x_ref[...]
    h = (x * _rms_rows(x) * g_ref[...]).astype(BF16)
    tm, cw = u_ref.shape

    def col(c):
        return jnp.dot(h, w_ref[:, c * cw:(c + 1) * cw], preferred_element_type=F32)

    bg_ref[...] = col(0)
    u_ref[...] = col(1) * col(2)
    q = _group_rmsnorm(col(3), bd_ref, gq_ref[...])
    k = _group_rmsnorm(col(4), bd_ref, gk_ref[...])
    v = col(5)
    kf_ref[...] = k
    vf_ref[...] = v
    for n, val in enumerate((q * ATTN_SCALE, k, v)):
        stage = stages[n]
        for c in range(cw // LANES):
            stage[c] = val[:, c * LANES:(c + 1) * LANES]
        for d, o_ref in zip(dilations, rest[n * n_d:(n + 1) * n_d]):
            for r in range(d):
                for c in range(cw // LANES):
                    lanes = slice(r * cw + c * LANES, r * cw + (c + 1) * LANES)
                    o_ref[:, lanes] = stage[c, pl.ds(r, tm // d, stride=d), :].astype(BF16)


def _inproj_even(x, g, w, qk_g, *, tm, dilations=(1,)):
    t, d_model = x.shape
    cw = w.shape[1] // 6
    reps = cw // HEAD_DIM
    row = lambda i: (i, 0)
    fix = lambda i: (0, 0)
    n_d = len(dilations)
    outs = [jax.ShapeDtypeStruct((t, cw), F32)] * 4 \
        + [jax.ShapeDtypeStruct((t // d, d * cw), BF16) for d in dilations] * 3
    out_specs = [pl.BlockSpec((tm, cw), row)] * 4 + [pl.BlockSpec((tm // d, d * cw), row) for d in dilations] * 3
    res = pl.pallas_call(
        functools.partial(_inproj_even_kernel, dilations=dilations),
        out_shape=outs,
        grid=(t // tm,),
        in_specs=[
            pl.BlockSpec((tm, d_model), row),
            pl.BlockSpec((1, d_model), fix),
            pl.BlockSpec(w.shape, fix),
            pl.BlockSpec((MXU_COLS, MXU_COLS), fix),
            pl.BlockSpec((1, cw), fix),
            pl.BlockSpec((1, cw), fix),
        ],
        out_specs=out_specs,
        scratch_shapes=[pltpu.VMEM((cw // LANES, tm, LANES), F32)] * 3,
        compiler_params=_params("parallel"),
        name="inproj_even",
    )(x, g.reshape(1, d_model), w, _group_ones(MXU_COLS, HEAD_DIM),
      jnp.tile(qk_g[0], reps).reshape(1, cw), jnp.tile(qk_g[1], reps).reshape(1, cw))
    return list(res[:4]) + [list(res[4 + n * n_d:4 + (n + 1) * n_d]) for n in range(3)]


def _inproj_odd_kernel(x_ref, g_ref, w_ref, bd_ref, gq_ref, gk_ref, *rest, kv_block):
    if kv_block:
        wvt_ref, bias_ref, qb_ref, ka_ref, kb_ref, kf_ref, vf_ref, vt_ref = rest
    else:
        qb_ref, kf_ref, vf_ref = rest
    x = x_ref[...]
    h = (x * _rms_rows(x) * g_ref[...]).astype(BF16)
    cw = qb_ref.shape[1]

    def col(c):
        return jnp.dot(h, w_ref[:, c * cw:(c + 1) * cw], preferred_element_type=F32)

    q = _group_rmsnorm(col(0), bd_ref, gq_ref[...])
    k = _group_rmsnorm(col(1), bd_ref, gk_ref[...])
    qb_ref[...] = (q * (ATTN_SCALE * LOG2E)).astype(BF16)
    kf_ref[...] = k
    vf_ref[...] = col(2)
    if kv_block:
        first_half = lax.broadcasted_iota(jnp.int32, (1, LANES), 1) < HEAD_DIM
        for hh in range(cw // LANES):
            sl = slice(hh * LANES, (hh + 1) * LANES)
            k_h, tab = k[:, sl], bias_ref[hh]
            ka_ref[:, sl] = jnp.where(first_half, k_h, tab).astype(BF16)
            kb_ref[:, sl] = jnp.where(first_half, pltpu.roll(k_h, HEAD_DIM, 1), tab).astype(BF16)
        vt = lax.dot_general(wvt_ref[...], h, (((1,), (1,)), ((), ())), preferred_element_type=F32).astype(BF16)
        for c in range(vt_ref.shape[0]):
            vt_ref[c] = vt[:, c * kv_block:(c + 1) * kv_block]


BIAS_LANES = 3


def _key_bias_table(n_heads, block):
    slope = jnp.asarray([_alibi_slope(h, n_heads) * LOG2E for h in range(n_heads)], F32)
    rest = slope[:, None] * jnp.arange(block, dtype=F32)[None, :]
    tab = jnp.zeros((n_heads, block, LANES), F32)
    for t in range(BIAS_LANES):
        bits = lax.bitcast_convert_type(rest, jnp.uint32) & jnp.uint32(0xFFFF0000)
        term = lax.bitcast_convert_type(bits, F32)
        tab = tab.at[:, :, HEAD_DIM + t].set(term)
        rest = rest - term
    return tab


def _inproj_odd(x, g, w, qk_g, *, tm, kv_block=0):
    t, d = x.shape
    cw = w.shape[1] // 3
    reps = cw // HEAD_DIM
    row = lambda i: (i, 0)
    fix = lambda i: (0, 0)
    ins = [x, g.reshape(1, d), w, _group_ones(MXU_COLS, HEAD_DIM),
           jnp.tile(qk_g[0], reps).reshape(1, cw), jnp.tile(qk_g[1], reps).reshape(1, cw)]
    in_specs = [pl.BlockSpec((tm, d), row), pl.BlockSpec((1, d), fix), pl.BlockSpec(w.shape, fix),
                pl.BlockSpec((MXU_COLS, MXU_COLS), fix), pl.BlockSpec((1, cw), fix), pl.BlockSpec((1, cw), fix)]
    tile = pl.BlockSpec((tm, cw), row)
    tok_bf16, tok_f32 = jax.ShapeDtypeStruct((t, cw), BF16), jax.ShapeDtypeStruct((t, cw), F32)
    if kv_block:
        assert tm == kv_block
        bias = _key_bias_table(cw // LANES, kv_block)
        ins += [w[:, 2 * cw:].T, bias]
        in_specs += [pl.BlockSpec((cw, d), fix), pl.BlockSpec(bias.shape, lambda i: (0, 0, 0))]
        outs = [tok_bf16, tok_bf16, tok_bf16, tok_f32, tok_f32,
                jax.ShapeDtypeStruct((t // kv_block, cw, kv_block), BF16)]
        out_specs = [tile] * 5 + [pl.BlockSpec((tm // kv_block, cw, kv_block), lambda i: (i, 0, 0))]
    else:
        outs = [tok_bf16, tok_f32, tok_f32]
        out_specs = [tile] * 3
    return pl.pallas_call(
        functools.partial(_inproj_odd_kernel, kv_block=kv_block),
        out_shape=outs,
        grid=(t // tm,),
        in_specs=in_specs,
        out_specs=out_specs,
        compiler_params=_params("parallel"),
        name="inproj_odd",
    )(*ins)


def _outproj_kernel(x_ref, a_ref, w_ref, o_ref):
    o_ref[...] = x_ref[...] + jnp.dot(a_ref[...].astype(BF16), w_ref[...], preferred_element_type=F32)


def _outproj(x, a, w, *, tm):
    t, d = x.shape
    row = lambda i: (i, 0)
    return pl.pallas_call(
        _outproj_kernel,
        out_shape=jax.ShapeDtypeStruct((t, d), F32),
        grid=(t // tm,),
        in_specs=[pl.BlockSpec((tm, d), row), pl.BlockSpec((tm, a.shape[1]), row),
                  pl.BlockSpec(w.shape, lambda i: (0, 0))],
        out_specs=pl.BlockSpec((tm, d), row),
        compiler_params=_params("parallel"),
        name="outproj",
    )(x, a, w)


def _alibi_slope(h, n_heads):
    return 2.0 ** (-8.0 * (h + 1) / n_heads)


def _dilated_prompt_kernel(q_ref, kc_ref, kp_ref, vc_ref, vp_ref, o_ref, lse_ref, *, dilation, n_heads):
    blk = pl.program_id(2)
    tq = q_ref.shape[1]
    qi = lax.broadcasted_iota(jnp.int32, (tq, 2 * tq), 0)
    kj = lax.broadcasted_iota(jnp.int32, (tq, 2 * tq), 1)
    dist = qi + tq - kj
    valid = (dist >= 0) & (dist <= SUB_WINDOW) & ((kj >= tq) | (blk > 0))
    ndist = (-dilation * dist).astype(F32)
    lane_hi = lax.broadcasted_iota(jnp.int32, (1, LANES), 1) >= HEAD_DIM
    for p in range(n_heads // 2):
        sl = slice(p * LANES, (p + 1) * LANES)
        q = q_ref[0, :, sl]
        k = jnp.concatenate([kp_ref[0, :, sl], kc_ref[0, :, sl]], axis=0)
        v = jnp.concatenate([vp_ref[0, :, sl], vc_ref[0, :, sl]], axis=0)
        o_pair = jnp.zeros((tq, LANES), F32)
        lse_pair = jnp.zeros((tq, LANES), F32)
        for e in range(2):
            keep = lane_hi if e else jnp.logical_not(lane_hi)
            qe = jnp.where(keep, q, jnp.zeros_like(q))
            ve = jnp.where(keep, v, jnp.zeros_like(v))
            s = lax.dot_general(qe, k, (((1,), (1,)), ((), ())), preferred_element_type=F32)
            s = jnp.where(valid, s + _alibi_slope(2 * p + e, n_heads) * ndist, NEG_INF)
            m = jnp.max(s, axis=-1, keepdims=True)
            pr = jnp.exp(s - m)
            l = jnp.sum(pr, axis=-1, keepdims=True)
            o_pair = o_pair + jnp.dot(pr.astype(BF16), ve, preferred_element_type=F32) * (1.0 / l)
            lse_pair = jnp.where(keep, m + jnp.log(l), lse_pair)
        o_ref[0, :, sl] = o_pair
        lse_ref[0, :, sl] = lse_pair


def _dilated_prompt(q, k, v, *, batch, dilation):
    d = dilation
    t, width = q.shape[0] * d, q.shape[1] // d
    s_len = t // batch
    tq = SUB_WINDOW
    nb = s_len // (d * tq)
    assert nb * d * tq == s_len
    view = lambda a: a.reshape(batch, s_len // d, d * width)
    cur = lambda b, r, i: (b, i, r)
    prev = lambda b, r, i: (b, jnp.maximum(i - 1, 0), r)
    blk = (1, tq, width)
    o, lse = pl.pallas_call(
        functools.partial(_dilated_prompt_kernel, dilation=d, n_heads=width // HEAD_DIM),
        out_shape=[jax.ShapeDtypeStruct((batch, s_len // d, d * width), F32)] * 2,
        grid=(batch, d, nb),
        in_specs=[pl.BlockSpec(blk, cur), pl.BlockSpec(blk, cur), pl.BlockSpec(blk, prev),
                  pl.BlockSpec(blk, cur), pl.BlockSpec(blk, prev)],
        out_specs=[pl.BlockSpec(blk, cur)] * 2,
        compiler_params=_params("parallel", "parallel", "arbitrary"),
        name=f"dilated_prompt_d{d}",
    )(view(q), view(k), view(k), view(v), view(v))
    return o.reshape(t // d, d * width), lse.reshape(t // d, d * width)


def _even_out_kernel(x_ref, bg_ref, u_ref, *rest, seq_rows, sample):
    if sample:
        s1_ref, s2_ref = rest[:2]
        rest = rest[2:]
    else:
        halo_ref = rest[0]
        rest = rest[1:]
    tm, cw = u_ref.shape
    if sample:
        o1_ref, cw_ref, w_ref, out_ref = rest
    else:
        n_d = len(DILATIONS)
        branch_refs, (cw_ref, w_ref, out_ref), stages = rest[:2 * n_d], rest[2 * n_d:2 * n_d + 3], rest[2 * n_d + 3:]

        def token_rows(n):
            d, g_ref = DILATIONS[n % n_d], branch_refs[n]
            if d == 1:
                return g_ref[...]
            stage = stages[n]
            for r in range(d):
                for c in range(cw // LANES):
                    lanes = slice(r * cw + c * LANES, r * cw + (c + 1) * LANES)
                    stage[c, pl.ds(r, tm // d, stride=d), :] = g_ref[:, lanes]
            return jnp.concatenate([stage[c] for c in range(cw // LANES)], axis=1)
    u = u_ref[...]
    r1 = pltpu.roll(u, 1, 0)
    r2 = pltpu.roll(u, 2, 0)
    if sample:
        t_in_seq = lax.broadcasted_iota(jnp.int32, (tm, cw), 0) % seq_rows
        u1 = jnp.where(t_in_seq >= 1, r1, s1_ref[...])
        u2 = jnp.where(t_in_seq >= 2, r2, s2_ref[...])
    else:
        first = (pl.program_id(0) % (seq_rows // tm)) == 0
        halo = jnp.where(first, 0.0, halo_ref[...])
        row8 = lax.broadcasted_iota(jnp.int32, halo.shape, 0)
        top1 = jnp.where(row8 < 1, pltpu.roll(halo, 1, 0), r1[0:8])
        top2 = jnp.where(row8 < 2, pltpu.roll(halo, 2, 0), r2[0:8])
        u1 = jnp.concatenate([top1, r1[8:]], axis=0)
        u2 = jnp.concatenate([top2, r2[8:]], axis=0)
    cwt = cw_ref[...]
    a = bg_ref[...] * (cwt[0:1] * u2 + cwt[1:2] * u1 + cwt[2:3] * u)
    if sample:
        b = o1_ref[...]
    else:
        o1, o2, o3, l1, l2, l3 = (token_rows(n) for n in range(2 * n_d))
        mx = jnp.maximum(jnp.maximum(l1, l2), l3)
        e1, e2, e3 = jnp.exp(l1 - mx), jnp.exp(l2 - mx), jnp.exp(l3 - mx)
        b = (e1 * o1 + e2 * o2 + e3 * o3) * (1.0 / (e1 + e2 + e3))
    out_ref[...] = (x_ref[...]
                    + jnp.dot(a.astype(BF16), w_ref[0:cw, :], preferred_element_type=F32)
                    + jnp.dot(b.astype(BF16), w_ref[cw:, :], preferred_element_type=F32))


def _even_out(x, bg, u, hist, outs, lses, conv_w, w_out, *, tm, seq_rows, sample):
    t, d = x.shape
    cw = u.shape[1]
    row = lambda i: (i, 0)
    fix = lambda i: (0, 0)
    tile = pl.BlockSpec((tm, cw), row)
    if sample:
        hist_in, hist_specs = list(hist), [tile, tile]
        branch_specs, scratch = [tile], []
    else:
        hist_in = [u]
        hist_specs = [pl.BlockSpec((8, cw), lambda i: (jnp.maximum(i * (tm // 8) - 1, 0), 0))]
        branch_specs = [pl.BlockSpec((tm // dil, dil * cw), row) for dil in DILATIONS] * 2
        scratch = [pltpu.VMEM((cw // LANES, tm, LANES), F32)] * (2 * len(DILATIONS))
    return pl.pallas_call(
        functools.partial(_even_out_kernel, seq_rows=seq_rows, sample=sample),
        out_shape=jax.ShapeDtypeStruct((t, d), F32),
        grid=(t // tm,),
        in_specs=[pl.BlockSpec((tm, d), row), tile, tile] + hist_specs + branch_specs
                 + [pl.BlockSpec(conv_w.shape, fix), pl.BlockSpec(w_out.shape, fix)],
        out_specs=pl.BlockSpec((tm, d), row),
        scratch_shapes=scratch,
        compiler_params=_params("parallel"),
        name="even_out",
    )(x, bg, u, *hist_in, *outs, *lses, conv_w, w_out)


def _diff_prompt_kernel(lam_ref, sub_ref, slope_ref, q_ref, ka_ref, kb_ref, vt_ref, o_ref,
                        qs_ref, sa_ref, sb_ref, m_ref, l_ref, acc_ref, *, lam_init):
    i = pl.program_id(2)
    tq = q_ref.shape[1]
    n_grp = 2 * tq // MXU_COLS
    q = q_ref[0].astype(F32)
    lane = lax.broadcasted_iota(jnp.int32, (1, LANES), 1)
    ones = jnp.where((lane >= HEAD_DIM) & (lane < HEAD_DIM + BIAS_LANES), 1.0, 0.0)
    qs_ref[0:tq, :] = jnp.where(lane < HEAD_DIM, q, ones).astype(BF16)
    qs_ref[tq:, :] = jnp.where(lane < HEAD_DIM, pltpu.roll(q, HEAD_DIM, 1), ones).astype(BF16)
    m_ref[...] = jnp.full_like(m_ref, NEG_INF)
    l_ref[...] = jnp.zeros_like(l_ref)
    acc_ref[...] = jnp.zeros_like(acc_ref)
    slope = slope_ref[0][:, 0:1]
    key_row = lax.broadcasted_iota(jnp.int32, (tq, LANES), 0)
    q_col = lax.broadcasted_iota(jnp.int32, (tq, LANES), 1)

    def score(j, s_ref):
        rows = pl.ds(pl.multiple_of(j * tq, tq), tq)
        for c in range(n_grp):
            cols = slice(c * MXU_COLS, (c + 1) * MXU_COLS)
            k_ref = ka_ref if c < n_grp // 2 else kb_ref
            s_ref[:, cols] = _nt(k_ref[0, rows, :], qs_ref[cols, :])

    def absorb(j, s_ref, diagonal):
        vt = vt_ref[j]
        off = slope * ((j - i) * tq).astype(F32)
        m_all, l_all = m_ref[...], l_ref[...]
        m_out, l_out = [], []
        for c in range(n_grp):
            cols = slice(c * MXU_COLS, (c + 1) * MXU_COLS)
            s = s_ref[:, cols]
            if diagonal:
                s = jnp.concatenate(
                    [jnp.where(key_row <= q_col + (c * MXU_COLS + cc * LANES) % tq,
                               s[:, cc * LANES:(cc + 1) * LANES], NEG_INF) for cc in range(MXU_COLS // LANES)], axis=1)
            m_prev = m_all[:, cols]
            m_new = jnp.maximum(m_prev, jnp.max(s, axis=0, keepdims=True) + off)
            alpha = jnp.exp2(m_prev - m_new)
            p = jnp.exp2(s - (m_new - off))
            l_out.append(alpha * l_all[:, cols] + jnp.sum(p, axis=0, keepdims=True))
            acc_ref[:, cols] = alpha * acc_ref[:, cols] + jnp.dot(vt, p.astype(BF16), preferred_element_type=F32)
            m_out.append(m_new)
        m_ref[...] = jnp.concatenate(m_out, axis=1)
        l_ref[...] = jnp.concatenate(l_out, axis=1)

    score(0, sa_ref)

    def pair(jj, carry):
        j = 2 * jj
        score(j + 1, sb_ref)
        absorb(j, sa_ref, False)
        score(j + 2, sa_ref)
        absorb(j + 1, sb_ref, False)
        return carry

    lax.fori_loop(0, i // 2, pair, 0)

    @pl.when(i % 2 == 1)
    def _():
        score(i, sb_ref)
        absorb(i - 1, sa_ref, False)
        absorb(i, sb_ref, True)

    @pl.when(i % 2 == 0)
    def _():
        absorb(i, sa_ref, True)

    lm = lam_ref[...]
    lam = (jnp.exp(jnp.sum(lm[0:1] * lm[1:2], axis=-1, keepdims=True))
           - jnp.exp(jnp.sum(lm[2:3] * lm[3:4], axis=-1, keepdims=True)) + lam_init)
    acc = acc_ref[...]
    inv = 1.0 / l_ref[...]
    o_t = acc[:, :tq] * inv[:, :tq] - lam * (acc[:, tq:] * inv[:, tq:])
    o = o_t.T
    o = o * _rms_rows(o) * sub_ref[...] * (1.0 - lam_init)
    o_ref[0] = o.astype(BF16)


def _diff_prompt(q, ka, kb, vt, lam, subln, *, batch, lam_init, tq):
    t, width = q.shape
    s_len = t // batch
    hw = 2 * HEAD_DIM
    n_heads = width // hw
    nq = s_len // tq
    assert vt.shape == (t // tq, width, tq) and nq * tq == s_len
    slopes = jnp.asarray([_alibi_slope(h, n_heads) * LOG2E for h in range(n_heads)], F32)
    slope_tab = jnp.broadcast_to(slopes[:, None, None], (n_heads, 1, LANES))
    out = pl.pallas_call(
        functools.partial(_diff_prompt_kernel, lam_init=lam_init),
        out_shape=jax.ShapeDtypeStruct((batch, s_len, width), BF16),
        grid=(batch, n_heads, nq),
        in_specs=[
            pl.BlockSpec(lam.shape, lambda b, h, i: (0, 0)),
            pl.BlockSpec((1, hw), lambda b, h, i: (0, 0)),
            pl.BlockSpec((1, 1, LANES), lambda b, h, i: (h, 0, 0)),
            pl.BlockSpec((1, tq, hw), lambda b, h, i: (b, i, h)),
            pl.BlockSpec((1, s_len, hw), lambda b, h, i: (b, 0, h)),
            pl.BlockSpec((1, s_len, hw), lambda b, h, i: (b, 0, h)),
            pl.BlockSpec((nq, hw, tq), lambda b, h, i: (b, h, 0)),
        ],
        out_specs=pl.BlockSpec((1, tq, hw), lambda b, h, i: (b, i, h)),
        scratch_shapes=[pltpu.VMEM((2 * tq, hw), BF16), pltpu.VMEM((tq, 2 * tq), F32), pltpu.VMEM((tq, 2 * tq), F32),
                        pltpu.VMEM((1, 2 * tq), F32), pltpu.VMEM((1, 2 * tq), F32), pltpu.VMEM((hw, 2 * tq), F32)],
        compiler_params=_params("parallel", "parallel", "arbitrary"),
        name="diff_prompt",
    )(lam, subln.reshape(1, hw), slope_tab, *(a.reshape(batch, s_len, width) for a in (q, ka, kb)), vt)
    return out.reshape(t, width)


SAMPLE_ROWS = 16


def _dilated_sample_tables(n_tok, w_buf):
    def tables(dist, valid):
        count = sum(((dist % d == 0) & (dist <= d * SUB_WINDOW)).astype(F32) for d in DILATIONS)
        held = valid & (count > 0)
        return (jnp.where(held, dist.astype(F32), -NEG_INF),
                jnp.where(held, jnp.log(jnp.maximum(count, 1.0)), 0.0))

    t = jnp.arange(SAMPLE_ROWS)[:, None]
    j = jnp.arange(LANES)[None, :]
    cache = tables(w_buf + t - jnp.arange(w_buf)[None, :], t < n_tok)
    new = tables(t - j, (t < n_tok) & (j <= t))
    return cache + new


def _dilated_sample_kernel(dc_ref, lc_ref, dn_ref, ln_ref, q_ref, kn_ref, vn_ref, kt_ref, vt_ref, o_ref):
    n_heads, n_tok, hd = q_ref.shape[1:]
    pad_q = jnp.zeros((SAMPLE_ROWS - n_tok, hd), F32)
    pad_n = jnp.zeros((LANES - n_tok, hd), F32)
    rmax = lambda a: jnp.max(a, axis=-1, keepdims=True)
    rsum = lambda a: jnp.sum(a, axis=-1, keepdims=True)
    for h in range(n_heads):
        slope = _alibi_slope(h, n_heads)
        q = jnp.concatenate([q_ref[0, h], pad_q], axis=0).astype(BF16)
        kn = jnp.concatenate([kn_ref[0, h], pad_n], axis=0).astype(BF16)
        vn = jnp.concatenate([vn_ref[0, h], pad_n], axis=0).astype(BF16)
        s = jnp.dot(q, kt_ref[0, 0, h].astype(BF16), preferred_element_type=F32)
        a = s - slope * dc_ref[...] + lc_ref[...]
        an = _nt(q, kn) - slope * dn_ref[...] + ln_ref[...]
        m = jnp.maximum(rmax(a), rmax(an))
        e = jnp.exp(a - m)
        en = jnp.exp(an - m)
        o = _nt(e.astype(BF16), vt_ref[0, 0, h].astype(BF16)) + jnp.dot(en.astype(BF16), vn,
                                                                        preferred_element_type=F32)
        o_ref[0, h] = (o * (1.0 / (rsum(e) + rsum(en))))[0:n_tok]


def _dilated_sample(q, k_new, v_new, cache_kt, cache_vt, layer_j, *, n_tok):
    n, n_heads, tok_pad, hd = q.shape
    w_buf = cache_kt.shape[-1]
    assert n_tok <= min(DILATIONS[1:]) and w_buf == max(DILATIONS) * SUB_WINDOW
    tabs = _dilated_sample_tables(n_tok, w_buf)
    fix = lambda i: (0, 0)
    tok = pl.BlockSpec((1, n_heads, tok_pad, hd), lambda i: (i, 0, 0, 0))
    blk = pl.BlockSpec((1, 1, n_heads, hd, w_buf), lambda i: (layer_j, i, 0, 0, 0))
    return pl.pallas_call(
        _dilated_sample_kernel,
        out_shape=jax.ShapeDtypeStruct((n, n_heads, tok_pad, hd), F32),
        grid=(n,),
        in_specs=[pl.BlockSpec(tb.shape, fix) for tb in tabs] + [tok, tok, tok, blk, blk],
        out_specs=tok,
        compiler_params=_params("parallel"),
        name="dilated_sample",
    )(*tabs, q, k_new, v_new, cache_kt, cache_vt)


def _diff_sample_tables(n_tok, n_heads):
    r = jnp.arange(2 * n_tok * n_heads)[:, None]
    t, h = (r // n_heads) % n_tok, r % n_heads
    slope = (jnp.exp2(-8.0 * (h + 1) / n_heads) * LOG2E).astype(F32)
    c = jnp.arange(PAGE_SIZE * n_heads)[None, :]
    tab_page = jnp.where(c % n_heads == h, slope * (c // n_heads - PAGE_SIZE).astype(F32), NEG_INF)
    cn = jnp.arange(LANES)[None, :]
    j = cn // n_heads
    tab_new = jnp.where((cn < n_tok * n_heads) & (cn % n_heads == h) & (j <= t), slope * j.astype(F32), NEG_INF)
    return tab_page, tab_new, slope


def _diff_sample_kernel(pt_ref, lam_ref, sub_ref, bp_ref, bn_ref, slope_ref, q_ref, kn_ref, vn_ref, *rest,
                        lam_init, group, n_pages):
    k_refs, v_refs = rest[:group], rest[group:2 * group]
    o_ref, w_ref, m_ref, l_ref, acc_ref = rest[2 * group:]
    g = pl.program_id(1)
    n_tok, n_heads, hw = q_ref.shape[1:]
    rows = n_tok * n_heads
    flat = lambda ref: ref[...].reshape(-1, hw)

    @pl.when(g == 0)
    def _():
        q = flat(q_ref)
        lane_hi = lax.broadcasted_iota(jnp.int32, (1, hw), 1) >= HEAD_DIM
        w_ref[0:rows, :] = jnp.where(lane_hi, 0.0, q).astype(BF16)
        w_ref[rows:, :] = jnp.where(lane_hi, q, 0.0).astype(BF16)
        m_ref[...] = jnp.full_like(m_ref, NEG_INF)
        l_ref[...] = jnp.zeros_like(l_ref)
        acc_ref[...] = jnp.zeros_like(acc_ref)

    def absorb(scores, offsets, values):
        m_prev = m_ref[...]
        m_new = m_prev
        for s, off in zip(scores, offsets):
            m_new = jnp.maximum(m_new, jnp.max(s, axis=-1, keepdims=True) + off)
        alpha = jnp.exp2(m_prev - m_new)
        l = alpha * l_ref[...]
        acc = alpha * acc_ref[...]
        for s, off, v in zip(scores, offsets, values):
            p = jnp.exp2(s - (m_new - off))
            l = l + jnp.sum(p, axis=-1, keepdims=True)
            acc = acc + jnp.dot(p.astype(BF16), v, preferred_element_type=F32)
        l_ref[...] = l
        acc_ref[...] = acc
        m_ref[...] = m_new

    slope = slope_ref[...]
    w = w_ref[...]
    absorb([_nt(w, flat(k_refs[i]).astype(BF16)) + bp_ref[...] for i in range(group)],
           [slope * ((g * group + i + 1 - n_pages) * PAGE_SIZE).astype(F32) for i in range(group)],
           [flat(v_refs[i]).astype(BF16) for i in range(group)])

    @pl.when(g == pl.num_programs(1) - 1)
    def _():
        pad = jnp.zeros((LANES - rows, hw), F32)
        kn = jnp.concatenate([flat(kn_ref), pad], axis=0).astype(BF16)
        vn = jnp.concatenate([flat(vn_ref), pad], axis=0).astype(BF16)
        absorb([_nt(w, kn) + bn_ref[...]], [0.0], [vn])
        lm = lam_ref[...]
        lam = (jnp.exp(jnp.sum(lm[0:1] * lm[1:2], axis=-1, keepdims=True))
               - jnp.exp(jnp.sum(lm[2:3] * lm[3:4], axis=-1, keepdims=True)) + lam_init)
        o = acc_ref[...] * (1.0 / l_ref[...])
        o = o[0:rows] - lam * o[rows:]
        o = o * _rms_rows(o) * sub_ref[...] * (1.0 - lam_init)
        o_ref[0] = o.reshape(n_tok, n_heads, hw)


def _diff_sample(q, k_new, v_new, cache_k, cache_v, page_table, lam, subln, layer_j, *, lam_init, group):
    n, n_tok, n_heads, hw = q.shape
    n_pages = page_table.shape[1]
    assert n_pages % group == 0
    rows = 2 * n_tok * n_heads
    tab_page, tab_new, slope = _diff_sample_tables(n_tok, n_heads)
    fix = lambda i, g, pt: (0, 0)
    tok = pl.BlockSpec((1, n_tok, n_heads, hw), lambda i, g, pt: (i, 0, 0, 0))

    def page_spec(slot):
        return pl.BlockSpec((1, 1, PAGE_SIZE, n_heads, hw),
                            lambda i, g, pt: (layer_j, pt[i, g * group + slot], 0, 0, 0))

    pages = [page_spec(s) for s in range(group)]
    return pl.pallas_call(
        functools.partial(_diff_sample_kernel, lam_init=lam_init, group=group, n_pages=n_pages),
        out_shape=jax.ShapeDtypeStruct((n, n_tok, n_heads, hw), F32),
        grid_spec=pltpu.PrefetchScalarGridSpec(
            num_scalar_prefetch=1,
            grid=(n, n_pages // group),
            in_specs=[pl.BlockSpec(lam.shape, fix), pl.BlockSpec((1, hw), fix), pl.BlockSpec(tab_page.shape, fix),
                      pl.BlockSpec(tab_new.shape, fix), pl.BlockSpec(slope.shape, fix), tok, tok, tok]
                     + pages + pages,
            out_specs=tok,
            scratch_shapes=[pltpu.VMEM((rows, hw), BF16), pltpu.VMEM((rows, 1), F32),
                            pltpu.VMEM((rows, 1), F32), pltpu.VMEM((rows, hw), F32)],
        ),
        compiler_params=_params("parallel", "arbitrary"),
        name="diff_sample",
    )(page_table, lam, subln.reshape(1, hw), tab_page, tab_new, slope, q, k_new, v_new,
      *([cache_k] * group), *([cache_v] * group))


def _row_tile(rows, target):
    tile = min(rows, target)
    while rows % tile:
        tile //= 2
    return tile


def kernel(x_prompt, x_sample, state_conv, cache_b_k, cache_b_v, cache_c_k, cache_c_v, page_table, norm_gain,
           ffn_w_gate, ffn_w_up, ffn_w_down, w_in_even, conv_w, qk_gain_b, w_out_even, w_in_odd, qk_gain_c,
           lambda_c, subln_c, w_out_odd):
    batch, seq, d = x_prompt.shape
    n, n_tok, _ = x_sample.shape
    depth = norm_gain.shape[0]
    w_buf = cache_b_k.shape[2]
    xp = x_prompt.reshape(batch * seq, d)
    xs = x_sample.reshape(n * n_tok, d)
    tp_ffn, tp, ts = _row_tile(batch * seq, 1024), _row_tile(batch * seq, 512), _row_tile(n * n_tok, 512)
    tq = _row_tile(seq, 512)
    wg, wu, wd = ffn_w_gate.astype(BF16), ffn_w_up.astype(BF16), ffn_w_down.astype(BF16)
    conv_p, conv_s, bk_p, bv_p, bk_s, bv_s, ck_p, cv_p, ck_s, cv_s = ([] for _ in range(10))

    for layer in range(depth):
        g = norm_gain[layer]
        j = layer // 2
        xp = _half_ffn(xp, g[0], wg, wu, wd, (layer, 0), tm=tp_ffn)
        xs = _half_ffn(xs, g[0], wg, wu, wd, (layer, 0), tm=ts)
        if layer % 2 == 0:
            w_in, w_out = w_in_even[j].astype(BF16), w_out_even[j].astype(BF16)
            cw = w_in.shape[1] // 6
            heads = (cw // HEAD_DIM, HEAD_DIM)
            bg, u, kf, vf, qs, ks, vs = _inproj_even(xp, g[1], w_in, qk_gain_b[j], tm=tp, dilations=DILATIONS)
            res = [_dilated_prompt(qs[b], ks[b], vs[b], batch=batch, dilation=dil) for b, dil in enumerate(DILATIONS)]
            xp = _even_out(xp, bg, u, None, [r[0] for r in res], [r[1] for r in res], conv_w[j], w_out,
                           tm=tp, seq_rows=seq, sample=False)
            conv_p.append(u.reshape(batch, seq, cw)[:, seq - (CONV_K - 1):])
            last = lambda a: a.reshape(batch, seq, cw)[:, seq - w_buf:].reshape(batch, w_buf, *heads)
            bk_p.append(last(kf))
            bv_p.append(last(vf))

            bg, u, kf, vf, (qb,), _, _ = _inproj_even(xs, g[1], w_in, qk_gain_b[j], tm=ts)
            shp = (n, n_tok) + heads
            tok_pad = -(-n_tok // 8) * 8

            def by_head(a):
                a = a.reshape(shp).transpose(0, 2, 1, 3)
                return jnp.pad(a, ((0, 0), (0, 0), (0, tok_pad - n_tok), (0, 0)))

            o = _dilated_sample(by_head(qb.astype(F32)), by_head(kf), by_head(vf),
                                cache_b_k.transpose(0, 1, 3, 4, 2), cache_b_v.transpose(0, 1, 3, 4, 2), j,
                                n_tok=n_tok)
            o = o[:, :, :n_tok].transpose(0, 2, 1, 3)
            st = state_conv[j]
            hist2 = jnp.pad(st, ((0, 0), (0, n_tok - (CONV_K - 1)), (0, 0))).reshape(n * n_tok, cw)
            hist1 = jnp.pad(st[:, 1:], ((0, 0), (0, n_tok - 1), (0, 0))).reshape(n * n_tok, cw)
            xs = _even_out(xs, bg, u, (hist1, hist2), [o.reshape(n * n_tok, cw)], [], conv_w[j], w_out,
                           tm=ts, seq_rows=n_tok, sample=True)
            conv_s.append(u.reshape(n, n_tok, cw)[:, n_tok - (CONV_K - 1):])
            bk_s.append(kf.reshape(shp))
            bv_s.append(vf.reshape(shp))
        else:
            w_in, w_out = w_in_odd[j].astype(BF16), w_out_odd[j].astype(BF16)
            cw = w_in.shape[1] // 3
            heads = (cw // (2 * HEAD_DIM), 2 * HEAD_DIM)
            lam_init = 0.8 - 0.6 * math.exp(-0.3 * layer)
            qb, ka, kb, kf, vf, vt = _inproj_odd(xp, g[1], w_in, qk_gain_c[j], tm=tq, kv_block=tq)
            o = _diff_prompt(qb, ka, kb, vt, lambda_c[j], subln_c[j], batch=batch, lam_init=lam_init, tq=tq)
            xp = _outproj(xp, o, w_out, tm=tp)
            ck_p.append(kf.reshape(batch, seq // PAGE_SIZE, PAGE_SIZE, *heads))
            cv_p.append(vf.reshape(batch, seq // PAGE_SIZE, PAGE_SIZE, *heads))

            qb, kf, vf = _inproj_odd(xs, g[1], w_in, qk_gain_c[j], tm=ts)
            shp = (n, n_tok) + heads
            o = _diff_sample(qb.astype(F32).reshape(shp), kf.reshape(shp), vf.reshape(shp), cache_c_k, cache_c_v,
                             page_table, lambda_c[j], subln_c[j], j, lam_init=lam_init,
                             group=math.gcd(page_table.shape[1], 8))
            xs = _outproj(xs, o.reshape(n * n_tok, cw), w_out, tm=ts)
            ck_s.append(kf.reshape(shp))
            cv_s.append(vf.reshape(shp))
        xp = _half_ffn(xp, g[2], wg, wu, wd, (layer, 1), tm=tp_ffn)
        xs = _half_ffn(xs, g[2], wg, wu, wd, (layer, 1), tm=ts)

    return (xp.reshape(batch, seq, d), xs.reshape(n, n_tok, d),
            jnp.stack(conv_p), jnp.stack(conv_s),
            jnp.stack(bk_p), jnp.stack(bv_p), jnp.stack(bk_s), jnp.stack(bv_s),
            jnp.stack(ck_p), jnp.stack(cv_p), jnp.stack(ck_s), jnp.stack(cv_s))
```

```python
import functools
import math

import jax
import jax.numpy as jnp
from jax import lax
from jax.experimental import pallas as pl
from jax.experimental.pallas import tpu as pltpu

F32 = jnp.float32
BF16 = jnp.bfloat16

HEAD_DIM = 64
CONV_K = 3
SUB_WINDOW = 128
DILATIONS = (1, 4, 16)
PAGE_SIZE = 128
ATTN_SCALE = HEAD_DIM ** -0.5
LOG2E = math.log2(math.e)
RMS_EPS = 1e-6
NEG_INF = -1e30

V7X_VMEM_LIMIT_BYTES = 56 * 1024 * 1024
MXU_COLS = 256
LANES = 128


def _params(*sem):
    return pltpu.CompilerParams(dimension_semantics=sem, vmem_limit_bytes=V7X_VMEM_LIMIT_BYTES)


def _rms_rows(x):
    return lax.rsqrt(jnp.mean(x * x, axis=-1, keepdims=True) + RMS_EPS)


def _nt(a, b):
    return lax.dot_general(a, b, (((1,), (1,)), ((), ())), preferred_element_type=F32)


def _group_ones(width, group):
    i = jnp.arange(width) // group
    return (i[:, None] == i[None, :]).astype(BF16)


def _group_rmsnorm(x, bd_ref, gain):
    x2 = x * x
    hi = x2.astype(BF16)
    lo = (x2 - hi.astype(F32)).astype(BF16)
    bd = bd_ref[...]
    cols = []
    for c in range(x.shape[1] // MXU_COLS):
        sl = slice(c * MXU_COLS, (c + 1) * MXU_COLS)
        cols.append(jnp.dot(hi[:, sl], bd, preferred_element_type=F32)
                    + jnp.dot(lo[:, sl], bd, preferred_element_type=F32))
    ss = jnp.concatenate(cols, axis=1)
    return x * lax.rsqrt(ss * (1.0 / HEAD_DIM) + RMS_EPS) * gain


def _ffn_kernel(x_ref, g_ref, wg_ref, wu_ref, wd_ref, o_ref, h_ref, acc_ref):
    j = pl.program_id(1)

    @pl.when(j == 0)
    def _():
        x = x_ref[...]
        h_ref[...] = (x * _rms_rows(x) * g_ref[...]).astype(BF16)
        acc_ref[...] = jnp.zeros_like(acc_ref)

    h = h_ref[...]
    a = jnp.dot(h, wg_ref[...], preferred_element_type=F32)
    b = jnp.dot(h, wu_ref[...], preferred_element_type=F32)
    z = (a * (1.0 / (1.0 + jnp.exp(-a))) * b).astype(BF16)
    acc_ref[...] += jnp.dot(z, wd_ref[...], preferred_element_type=F32)

    @pl.when(j == pl.num_programs(1) - 1)
    def _():
        o_ref[...] = x_ref[...] + 0.5 * acc_ref[...]


def _half_ffn(x, g, wg, wu, wd, which, *, tm, tf=MXU_COLS):
    t, d = x.shape
    f = wg.shape[-1]
    return pl.pallas_call(
        _ffn_kernel,
        out_shape=jax.ShapeDtypeStruct((t, d), F32),
        grid=(t // tm, f // tf),
        in_specs=[
            pl.BlockSpec((tm, d), lambda i, j: (i, 0)),
            pl.BlockSpec((1, d), lambda i, j: (0, 0)),
            pl.BlockSpec((None, None, d, tf), lambda i, j: (*which, 0, j)),
            pl.BlockSpec((None, None, d, tf), lambda i, j: (*which, 0, j)),
            pl.BlockSpec((None, None, tf, d), lambda i, j: (*which, j, 0)),
        ],
        out_specs=pl.BlockSpec((tm, d), lambda i, j: (i, 0)),
        scratch_shapes=[pltpu.VMEM((tm, d), BF16), pltpu.VMEM((tm, d), F32)],
        compiler_params=_params("parallel", "arbitrary"),
        name="half_ffn",
    )(x, g.reshape(1, d), wg, wu, wd)


def _inproj_even_kernel(x_ref, g_ref, w_ref, bd_ref, gq_ref, gk_ref, bg_ref, u_ref, kf_ref, vf_ref, *rest, dilations):
    n_d = len(dilations)
    stages = rest[3 * n_d:]
    x = x_ref[...]
    h = (x * _rms_rows(x) * g_ref[...]).astype(BF16)
    tm, cw = u_ref.shape

    def col(c):
        return jnp.dot(h, w_ref[:, c * cw:(c + 1) * cw], preferred_element_type=F32)

    bg_ref[...] = col(0)
    u_ref[...] = col(1) * col(2)
    q = _group_rmsnorm(col(3), bd_ref, gq_ref[...])
    k = _group_rmsnorm(col(4), bd_ref, gk_ref[...])
    v = col(5)
    kf_ref[...] = k
    vf_ref[...] = v
    for n, val in enumerate((q * ATTN_SCALE, k, v)):
        stage = stages[n]
        for c in range(cw // LANES):
            stage[c] = val[:, c * LANES:(c + 1) * LANES]
        for d, o_ref in zip(dilations, rest[n * n_d:(n + 1) * n_d]):
            for r in range(d):
                for c in range(cw // LANES):
                    lanes = slice(r * cw + c * LANES, r * cw + (c + 1) * LANES)
                    o_ref[:, lanes] = stage[c, pl.ds(r, tm // d, stride=d), :].astype(BF16)


def _inproj_even(x, g, w, qk_g, *, tm, dilations=(1,)):
    t, d_model = x.shape
    cw = w.shape[1] // 6
    reps = cw // HEAD_DIM
    row = lambda i: (i, 0)
    fix = lambda i: (0, 0)
    n_d = len(dilations)
    outs = [jax.ShapeDtypeStruct((t, cw), F32)] * 4 \
        + [jax.ShapeDtypeStruct((t // d, d * cw), BF16) for d in dilations] * 3
    out_specs = [pl.BlockSpec((tm, cw), row)] * 4 + [pl.BlockSpec((tm // d, d * cw), row) for d in dilations] * 3
    res = pl.pallas_call(
        functools.partial(_inproj_even_kernel, dilations=dilations),
        out_shape=outs,
        grid=(t // tm,),
        in_specs=[
            pl.BlockSpec((tm, d_model), row),
            pl.BlockSpec((1, d_model), fix),
            pl.BlockSpec(w.shape, fix),
            pl.BlockSpec((MXU_COLS, MXU_COLS), fix),
            pl.BlockSpec((1, cw), fix),
            pl.BlockSpec((1, cw), fix),
        ],
        out_specs=out_specs,
        scratch_shapes=[pltpu.VMEM((cw // LANES, tm, LANES), F32)] * 3,
        compiler_params=_params("parallel"),
        name="inproj_even",
    )(x, g.reshape(1, d_model), w, _group_ones(MXU_COLS, HEAD_DIM),
      jnp.tile(qk_g[0], reps).reshape(1, cw), jnp.tile(qk_g[1], reps).reshape(1, cw))
    return list(res[:4]) + [list(res[4 + n * n_d:4 + (n + 1) * n_d]) for n in range(3)]


def _inproj_odd_kernel(x_ref, g_ref, w_ref, bd_ref, gq_ref, gk_ref, *rest, kv_block):
    if kv_block:
        bias_ref, qb_ref, ka_ref, kb_ref, kf_ref, vf_ref, vt_ref = rest
    else:
        qb_ref, kf_ref, vf_ref = rest
    x = x_ref[...]
    h = (x * _rms_rows(x) * g_ref[...]).astype(BF16)
    cw = qb_ref.shape[1]

    def col(c):
        return jnp.dot(h, w_ref[:, c * cw:(c + 1) * cw], preferred_element_type=F32)

    q = _group_rmsnorm(col(0), bd_ref, gq_ref[...])
    k = _group_rmsnorm(col(1), bd_ref, gk_ref[...])
    qb_ref[...] = (q * (ATTN_SCALE * LOG2E)).astype(BF16)
    kf_ref[...] = k
    v = col(2)
    vf_ref[...] = v
    if kv_block:
        first_half = lax.broadcasted_iota(jnp.int32, (1, LANES), 1) < HEAD_DIM
        for hh in range(cw // LANES):
            sl = slice(hh * LANES, (hh + 1) * LANES)
            k_h, tab = k[:, sl], bias_ref[hh]
            ka_ref[:, sl] = jnp.where(first_half, k_h, tab).astype(BF16)
            kb_ref[:, sl] = jnp.where(first_half, pltpu.roll(k_h, HEAD_DIM, 1), tab).astype(BF16)
        vt = v.T.astype(BF16)
        for c in range(vt_ref.shape[0]):
            vt_ref[c] = vt[:, c * kv_block:(c + 1) * kv_block]


BIAS_LANES = 3


def _key_bias_table(n_heads, block):
    slope = jnp.asarray([_alibi_slope(h, n_heads) * LOG2E for h in range(n_heads)], F32)
    rest = slope[:, None] * jnp.arange(block, dtype=F32)[None, :]
    tab = jnp.zeros((n_heads, block, LANES), F32)
    for t in range(BIAS_LANES):
        bits = lax.bitcast_convert_type(rest, jnp.uint32) & jnp.uint32(0xFFFF0000)
        term = lax.bitcast_convert_type(bits, F32)
        tab = tab.at[:, :, HEAD_DIM + t].set(term)
        rest = rest - term
    return tab


def _inproj_odd(x, g, w, qk_g, *, tm, kv_block=0):
    t, d = x.shape
    cw = w.shape[1] // 3
    reps = cw // HEAD_DIM
    row = lambda i: (i, 0)
    fix = lambda i: (0, 0)
    ins = [x, g.reshape(1, d), w, _group_ones(MXU_COLS, HEAD_DIM),
           jnp.tile(qk_g[0], reps).reshape(1, cw), jnp.tile(qk_g[1], reps).reshape(1, cw)]
    in_specs = [pl.BlockSpec((tm, d), row), pl.BlockSpec((1, d), fix), pl.BlockSpec(w.shape, fix),
                pl.BlockSpec((MXU_COLS, MXU_COLS), fix), pl.BlockSpec((1, cw), fix), pl.BlockSpec((1, cw), fix)]
    tile = pl.BlockSpec((tm, cw), row)
    tok_bf16, tok_f32 = jax.ShapeDtypeStruct((t, cw), BF16), jax.ShapeDtypeStruct((t, cw), F32)
    if kv_block:
        assert tm == kv_block
        bias = _key_bias_table(cw // LANES, kv_block)
        ins.append(bias)
        in_specs.append(pl.BlockSpec(bias.shape, lambda i: (0, 0, 0)))
        outs = [tok_bf16, tok_bf16, tok_bf16, tok_f32, tok_f32,
                jax.ShapeDtypeStruct((t // kv_block, cw, kv_block), BF16)]
        out_specs = [tile] * 5 + [pl.BlockSpec((tm // kv_block, cw, kv_block), lambda i: (i, 0, 0))]
    else:
        outs = [tok_bf16, tok_f32, tok_f32]
        out_specs = [tile] * 3
    return pl.pallas_call(
        functools.partial(_inproj_odd_kernel, kv_block=kv_block),
        out_shape=outs,
        grid=(t // tm,),
        in_specs=in_specs,
        out_specs=out_specs,
        compiler_params=_params("parallel"),
        name="inproj_odd",
    )(*ins)


def _outproj_kernel(x_ref, a_ref, w_ref, o_ref):
    o_ref[...] = x_ref[...] + jnp.dot(a_ref[...].astype(BF16), w_ref[...], preferred_element_type=F32)


def _outproj(x, a, w, *, tm):
    t, d = x.shape
    row = lambda i: (i, 0)
    return pl.pallas_call(
        _outproj_kernel,
        out_shape=jax.ShapeDtypeStruct((t, d), F32),
        grid=(t // tm,),
        in_specs=[pl.BlockSpec((tm, d), row), pl.BlockSpec((tm, a.shape[1]), row),
                  pl.BlockSpec(w.shape, lambda i: (0, 0))],
        out_specs=pl.BlockSpec((tm, d), row),
        compiler_params=_params("parallel"),
        name="outproj",
    )(x, a, w)


def _alibi_slope(h, n_heads):
    return 2.0 ** (-8.0 * (h + 1) / n_heads)


def _dilated_prompt_kernel(bias_ref, q_ref, kc_ref, kp_ref, vc_ref, vp_ref, o_ref, lse_ref, *, n_heads):
    step = pl.program_id(2)
    tq = kp_ref.shape[1]
    n_sub = q_ref.shape[1] // tq
    lane_hi = lax.broadcasted_iota(jnp.int32, (1, LANES), 1) >= HEAD_DIM
    for sub in range(n_sub):
        rows = slice(sub * tq, (sub + 1) * tq)
        before = slice((sub - 1) * tq, sub * tq)
        variant = jnp.minimum(step, 1) if sub == 0 else 1
        for p in range(n_heads // 2):
            sl = slice(p * LANES, (p + 1) * LANES)
            q = q_ref[0, rows, sl]
            k_prev, v_prev = (kp_ref[0, :, sl], vp_ref[0, :, sl]) if sub == 0 else (kc_ref[0, before, sl],
                                                                                   vc_ref[0, before, sl])
            k = jnp.concatenate([k_prev, kc_ref[0, rows, sl]], axis=0)
            v = jnp.concatenate([v_prev, vc_ref[0, rows, sl]], axis=0)
            o_pair = jnp.zeros((tq, LANES), F32)
            lse_pair = jnp.zeros((tq, LANES), F32)
            for e in range(2):
                keep = lane_hi if e else jnp.logical_not(lane_hi)
                qe = jnp.where(keep, q, jnp.zeros_like(q))
                ve = jnp.where(keep, v, jnp.zeros_like(v))
                s = _nt(qe, k) + bias_ref[variant, 2 * p + e]
                m = jnp.max(s, axis=-1, keepdims=True)
                pr = jnp.exp(s - m)
                l = jnp.sum(pr, axis=-1, keepdims=True)
                o_pair = o_pair + jnp.dot(pr.astype(BF16), ve, preferred_element_type=F32) * (1.0 / l)
                lse_pair = jnp.where(keep, m + jnp.log(l), lse_pair)
            o_ref[0, rows, sl] = o_pair
            lse_ref[0, rows, sl] = lse_pair


def _dilated_bias_table(n_heads, dilation):
    tq = SUB_WINDOW
    qi = jnp.arange(tq)[:, None]
    kj = jnp.arange(2 * tq)[None, :]
    dist = qi + tq - kj
    band = (dist >= 0) & (dist <= SUB_WINDOW)
    slope = jnp.asarray([_alibi_slope(h, n_heads) for h in range(n_heads)], F32)[:, None, None]
    bias = -slope * (dilation * dist).astype(F32)[None]
    return jnp.stack([jnp.where(band & (kj >= tq), bias, NEG_INF), jnp.where(band, bias, NEG_INF)])


def _dilated_prompt(q, k, v, *, batch, dilation):
    d = dilation
    t, width = q.shape[0] * d, q.shape[1] // d
    s_len = t // batch
    tq = SUB_WINDOW
    n_sub = 2 if s_len % (2 * d * tq) == 0 else 1
    steps = s_len // (d * tq * n_sub)
    assert steps * d * tq * n_sub == s_len
    n_heads = width // HEAD_DIM
    bias = _dilated_bias_table(n_heads, d)
    view = lambda a: a.reshape(batch, s_len // d, d * width)
    cur = lambda b, r, i: (b, i, r)
    prev = lambda b, r, i: (b, jnp.maximum(i * n_sub - 1, 0), r)
    blk, halo = (1, n_sub * tq, width), (1, tq, width)
    o, lse = pl.pallas_call(
        functools.partial(_dilated_prompt_kernel, n_heads=n_heads),
        out_shape=[jax.ShapeDtypeStruct((batch, s_len // d, d * width), F32)] * 2,
        grid=(batch, d, steps),
        in_specs=[pl.BlockSpec(bias.shape, lambda b, r, i: (0, 0, 0, 0)),
                  pl.BlockSpec(blk, cur), pl.BlockSpec(blk, cur), pl.BlockSpec(halo, prev),
                  pl.BlockSpec(blk, cur), pl.BlockSpec(halo, prev)],
        out_specs=[pl.BlockSpec(blk, cur)] * 2,
        compiler_params=_params("parallel", "parallel", "arbitrary"),
        name=f"dilated_prompt_d{d}",
    )(bias, view(q), view(k), view(k), view(v), view(v))
    return o.reshape(t // d, d * width), lse.reshape(t // d, d * width)


def _even_out_kernel(x_ref, bg_ref, u_ref, *rest, seq_rows, sample):
    if sample:
        s1_ref, s2_ref = rest[:2]
        rest = rest[2:]
    else:
        halo_ref = rest[0]
        rest = rest[1:]
    tm, cw = u_ref.shape
    if sample:
        o1_ref, cw_ref, w_ref, out_ref = rest
    else:
        n_d = len(DILATIONS)
        branch_refs, (cw_ref, w_ref, out_ref), stages = rest[:2 * n_d], rest[2 * n_d:2 * n_d + 3], rest[2 * n_d + 3:]

        def token_rows(n):
            d, g_ref = DILATIONS[n % n_d], branch_refs[n]
            if d == 1:
                return g_ref[...]
            stage = stages[n]
            for r in range(d):
                for c in range(cw // LANES):
                    lanes = slice(r * cw + c * LANES, r * cw + (c + 1) * LANES)
                    stage[c, pl.ds(r, tm // d, stride=d), :] = g_ref[:, lanes]
            return jnp.concatenate([stage[c] for c in range(cw // LANES)], axis=1)
    u = u_ref[...]
    r1 = pltpu.roll(u, 1, 0)
    r2 = pltpu.roll(u, 2, 0)
    if sample:
        t_in_seq = lax.broadcasted_iota(jnp.int32, (tm, cw), 0) % seq_rows
        u1 = jnp.where(t_in_seq >= 1, r1, s1_ref[...])
        u2 = jnp.where(t_in_seq >= 2, r2, s2_ref[...])
    else:
        first = (pl.program_id(0) % (seq_rows // tm)) == 0
        halo = jnp.where(first, 0.0, halo_ref[...])
        row8 = lax.broadcasted_iota(jnp.int32, halo.shape, 0)
        top1 = jnp.where(row8 < 1, pltpu.roll(halo, 1, 0), r1[0:8])
        top2 = jnp.where(row8 < 2, pltpu.roll(halo, 2, 0), r2[0:8])
        u1 = jnp.concatenate([top1, r1[8:]], axis=0)
        u2 = jnp.concatenate([top2, r2[8:]], axis=0)
    cwt = cw_ref[...]
    a = bg_ref[...] * (cwt[0:1] * u2 + cwt[1:2] * u1 + cwt[2:3] * u)
    if sample:
        b = o1_ref[...]
    else:
        o1, o2, o3, l1, l2, l3 = (token_rows(n) for n in range(2 * n_d))
        mx = jnp.maximum(jnp.maximum(l1, l2), l3)
        e1, e2, e3 = jnp.exp(l1 - mx), jnp.exp(l2 - mx), jnp.exp(l3 - mx)
        b = (e1 * o1 + e2 * o2 + e3 * o3) * (1.0 / (e1 + e2 + e3))
    out_ref[...] = (x_ref[...]
                    + jnp.dot(a.astype(BF16), w_ref[0:cw, :], preferred_element_type=F32)
                    + jnp.dot(b.astype(BF16), w_ref[cw:, :], preferred_element_type=F32))


def _even_out(x, bg, u, hist, outs, lses, conv_w, w_out, *, tm, seq_rows, sample):
    t, d = x.shape
    cw = u.shape[1]
    row = lambda i: (i, 0)
    fix = lambda i: (0, 0)
    tile = pl.BlockSpec((tm, cw), row)
    if sample:
        hist_in, hist_specs = list(hist), [tile, tile]
        branch_specs, scratch = [tile], []
    else:
        hist_in = [u]
        hist_specs = [pl.BlockSpec((8, cw), lambda i: (jnp.maximum(i * (tm // 8) - 1, 0), 0))]
        branch_specs = [pl.BlockSpec((tm // dil, dil * cw), row) for dil in DILATIONS] * 2
        scratch = [pltpu.VMEM((cw // LANES, tm, LANES), F32)] * (2 * len(DILATIONS))
    return pl.pallas_call(
        functools.partial(_even_out_kernel, seq_rows=seq_rows, sample=sample),
        out_shape=jax.ShapeDtypeStruct((t, d), F32),
        grid=(t // tm,),
        in_specs=[pl.BlockSpec((tm, d), row), tile, tile] + hist_specs + branch_specs
                 + [pl.BlockSpec(conv_w.shape, fix), pl.BlockSpec(w_out.shape, fix)],
        out_specs=pl.BlockSpec((tm, d), row),
        scratch_shapes=scratch,
        compiler_params=_params("parallel"),
        name="even_out",
    )(x, bg, u, *hist_in, *outs, *lses, conv_w, w_out)


def _diff_prompt_kernel(lam_ref, sub_ref, slope_ref, q_ref, ka_ref, kb_ref, vt_ref, o_ref,
                        qs_ref, sa_ref, sb_ref, ma_ref, mb_ref, m_ref, l_ref, acc_ref, *, lam_init):
    i = pl.program_id(2)
    tq = q_ref.shape[1]
    n_grp = 2 * tq // MXU_COLS
    q = q_ref[0].astype(F32)
    lane = lax.broadcasted_iota(jnp.int32, (1, LANES), 1)
    ones = jnp.where((lane >= HEAD_DIM) & (lane < HEAD_DIM + BIAS_LANES), 1.0, 0.0)
    qs_ref[0:tq, :] = jnp.where(lane < HEAD_DIM, q, ones).astype(BF16)
    qs_ref[tq:, :] = jnp.where(lane < HEAD_DIM, pltpu.roll(q, HEAD_DIM, 1), ones).astype(BF16)
    m_ref[...] = jnp.full_like(m_ref, NEG_INF)
    l_ref[...] = jnp.zeros_like(l_ref)
    acc_ref[...] = jnp.zeros_like(acc_ref)
    slope = slope_ref[0][:, 0:1]
    key_row = lax.broadcasted_iota(jnp.int32, (tq, LANES), 0)
    q_col = lax.broadcasted_iota(jnp.int32, (tq, LANES), 1)

    buf_a, buf_b = (sa_ref, ma_ref), (sb_ref, mb_ref)

    def score(j, buf):
        s_ref, smax_ref = buf
        rows = pl.ds(pl.multiple_of(j * tq, tq), tq)
        for c in range(n_grp):
            cols = slice(c * MXU_COLS, (c + 1) * MXU_COLS)
            k_ref = ka_ref if c < n_grp // 2 else kb_ref
            s = _nt(k_ref[0, rows, :], qs_ref[cols, :])
            s_ref[:, cols] = s
            smax_ref[:, cols] = jnp.max(s, axis=0, keepdims=True)

    def absorb(j, buf, diagonal):
        s_ref, smax_ref = buf
        vt = vt_ref[j]
        off = slope * ((j - i) * tq).astype(F32)
        m_all, l_all = m_ref[...], l_ref[...]
        m_out, l_out = [], []
        for c in range(n_grp):
            cols = slice(c * MXU_COLS, (c + 1) * MXU_COLS)
            s = s_ref[:, cols]
            if diagonal:
                s = jnp.concatenate(
                    [jnp.where(key_row <= q_col + (c * MXU_COLS + cc * LANES) % tq,
                               s[:, cc * LANES:(cc + 1) * LANES], NEG_INF) for cc in range(MXU_COLS // LANES)], axis=1)
                smax = jnp.max(s, axis=0, keepdims=True)
            else:
                smax = smax_ref[:, cols]
            m_prev = m_all[:, cols]
            m_new = jnp.maximum(m_prev, smax + off)
            alpha = jnp.exp2(m_prev - m_new)
            p = jnp.exp2(s - (m_new - off))
            l_out.append(alpha * l_all[:, cols] + jnp.sum(p, axis=0, keepdims=True))
            acc_ref[:, cols] = alpha * acc_ref[:, cols] + jnp.dot(vt, p.astype(BF16), preferred_element_type=F32)
            m_out.append(m_new)
        m_ref[...] = jnp.concatenate(m_out, axis=1)
        l_ref[...] = jnp.concatenate(l_out, axis=1)

    score(0, buf_a)

    def pair(jj, carry):
        j = 2 * jj
        score(j + 1, buf_b)
        absorb(j, buf_a, False)
        score(j + 2, buf_a)
        absorb(j + 1, buf_b, False)
        return carry

    lax.fori_loop(0, i // 2, pair, 0)

    @pl.when(i % 2 == 1)
    def _():
        score(i, buf_b)
        absorb(i - 1, buf_a, False)
        absorb(i, buf_b, True)

    @pl.when(i % 2 == 0)
    def _():
        absorb(i, buf_a, True)

    lm = lam_ref[...]
    lam = (jnp.exp(jnp.sum(lm[0:1] * lm[1:2], axis=-1, keepdims=True))
           - jnp.exp(jnp.sum(lm[2:3] * lm[3:4], axis=-1, keepdims=True)) + lam_init)
    acc = acc_ref[...]
    inv = 1.0 / l_ref[...]
    o_t = acc[:, :tq] * inv[:, :tq] - lam * (acc[:, tq:] * inv[:, tq:])
    o = o_t.T
    o = o * _rms_rows(o) * sub_ref[...] * (1.0 - lam_init)
    o_ref[0] = o.astype(BF16)


def _diff_prompt(q, ka, kb, vt, lam, subln, *, batch, lam_init, tq):
    t, width = q.shape
    s_len = t // batch
    hw = 2 * HEAD_DIM
    n_heads = width // hw
    nq = s_len // tq
    assert vt.shape == (t // tq, width, tq) and nq * tq == s_len
    slopes = jnp.asarray([_alibi_slope(h, n_heads) * LOG2E for h in range(n_heads)], F32)
    slope_tab = jnp.broadcast_to(slopes[:, None, None], (n_heads, 1, LANES))
    out = pl.pallas_call(
        functools.partial(_diff_prompt_kernel, lam_init=lam_init),
        out_shape=jax.ShapeDtypeStruct((batch, s_len, width), BF16),
        grid=(batch, n_heads, nq),
        in_specs=[
            pl.BlockSpec(lam.shape, lambda b, h, i: (0, 0)),
            pl.BlockSpec((1, hw), lambda b, h, i: (0, 0)),
            pl.BlockSpec((1, 1, LANES), lambda b, h, i: (h, 0, 0)),
            pl.BlockSpec((1, tq, hw), lambda b, h, i: (b, i, h)),
            pl.BlockSpec((1, s_len, hw), lambda b, h, i: (b, 0, h)),
            pl.BlockSpec((1, s_len, hw), lambda b, h, i: (b, 0, h)),
            pl.BlockSpec((nq, hw, tq), lambda b, h, i: (b, h, 0)),
        ],
        out_specs=pl.BlockSpec((1, tq, hw), lambda b, h, i: (b, i, h)),
        scratch_shapes=[pltpu.VMEM((2 * tq, hw), BF16), pltpu.VMEM((tq, 2 * tq), F32), pltpu.VMEM((tq, 2 * tq), F32)]
                       + [pltpu.VMEM((1, 2 * tq), F32)] * 4 + [pltpu.VMEM((hw, 2 * tq), F32)],
        compiler_params=_params("parallel", "parallel", "arbitrary"),
        name="diff_prompt",
    )(lam, subln.reshape(1, hw), slope_tab, *(a.reshape(batch, s_len, width) for a in (q, ka, kb)), vt)
    return out.reshape(t, width)


SAMPLE_ROWS = 16


def _dilated_sample_tables(n_tok, w_buf):
    def tables(dist, valid):
        count = sum(((dist % d == 0) & (dist <= d * SUB_WINDOW)).astype(F32) for d in DILATIONS)
        held = valid & (count > 0)
        return (jnp.where(held, dist.astype(F32), -NEG_INF),
                jnp.where(held, jnp.log(jnp.maximum(count, 1.0)), 0.0))

    t = jnp.arange(SAMPLE_ROWS)[:, None]
    j = jnp.arange(LANES)[None, :]
    cache = tables(w_buf + t - jnp.arange(w_buf)[None, :], t < n_tok)
    new = tables(t - j, (t < n_tok) & (j <= t))
    return cache + new


def _dilated_sample_kernel(dc_ref, lc_ref, dn_ref, ln_ref, q_ref, kn_ref, vn_ref, kt_ref, vt_ref, o_ref):
    n_heads, n_tok, hd = q_ref.shape[1:]
    pad_q = jnp.zeros((SAMPLE_ROWS - n_tok, hd), F32)
    pad_n = jnp.zeros((LANES - n_tok, hd), F32)
    rmax = lambda a: jnp.max(a, axis=-1, keepdims=True)
    rsum = lambda a: jnp.sum(a, axis=-1, keepdims=True)
    for h in range(n_heads):
        slope = _alibi_slope(h, n_heads)
        q = jnp.concatenate([q_ref[0, h], pad_q], axis=0).astype(BF16)
        kn = jnp.concatenate([kn_ref[0, h], pad_n], axis=0).astype(BF16)
        vn = jnp.concatenate([vn_ref[0, h], pad_n], axis=0).astype(BF16)
        s = jnp.dot(q, kt_ref[0, 0, h].astype(BF16), preferred_element_type=F32)
        a = s - slope * dc_ref[...] + lc_ref[...]
        an = _nt(q, kn) - slope * dn_ref[...] + ln_ref[...]
        m = jnp.maximum(rmax(a), rmax(an))
        e = jnp.exp(a - m)
        en = jnp.exp(an - m)
        o = _nt(e.astype(BF16), vt_ref[0, 0, h].astype(BF16)) + jnp.dot(en.astype(BF16), vn,
                                                                        preferred_element_type=F32)
        o_ref[0, h] = (o * (1.0 / (rsum(e) + rsum(en))))[0:n_tok]


def _dilated_sample(q, k_new, v_new, cache_kt, cache_vt, layer_j, *, n_tok):
    n, n_heads, tok_pad, hd = q.shape
    w_buf = cache_kt.shape[-1]
    assert n_tok <= min(DILATIONS[1:]) and w_buf == max(DILATIONS) * SUB_WINDOW
    tabs = _dilated_sample_tables(n_tok, w_buf)
    fix = lambda i: (0, 0)
    tok = pl.BlockSpec((1, n_heads, tok_pad, hd), lambda i: (i, 0, 0, 0))
    blk = pl.BlockSpec((1, 1, n_heads, hd, w_buf), lambda i: (layer_j, i, 0, 0, 0))
    return pl.pallas_call(
        _dilated_sample_kernel,
        out_shape=jax.ShapeDtypeStruct((n, n_heads, tok_pad, hd), F32),
        grid=(n,),
        in_specs=[pl.BlockSpec(tb.shape, fix) for tb in tabs] + [tok, tok, tok, blk, blk],
        out_specs=tok,
        compiler_params=_params("parallel"),
        name="dilated_sample",
    )(*tabs, q, k_new, v_new, cache_kt, cache_vt)


def _diff_sample_tables(n_tok, n_heads):
    r = jnp.arange(2 * n_tok * n_heads)[:, None]
    t, h = (r // n_heads) % n_tok, r % n_heads
    slope = (jnp.exp2(-8.0 * (h + 1) / n_heads) * LOG2E).astype(F32)
    c = jnp.arange(PAGE_SIZE * n_heads)[None, :]
    tab_page = jnp.where(c % n_heads == h, slope * (c // n_heads - PAGE_SIZE).astype(F32), NEG_INF)
    cn = jnp.arange(LANES)[None, :]
    j = cn // n_heads
    tab_new = jnp.where((cn < n_tok * n_heads) & (cn % n_heads == h) & (j <= t), slope * j.astype(F32), NEG_INF)
    return tab_page, tab_new, slope


def _diff_sample_kernel(pt_ref, lam_ref, sub_ref, bp_ref, bn_ref, slope_ref, q_ref, kn_ref, vn_ref, *rest,
                        lam_init, group, n_pages):
    k_refs, v_refs = rest[:group], rest[group:2 * group]
    o_ref, w_ref, m_ref, l_ref, acc_ref = rest[2 * group:]
    g = pl.program_id(1)
    n_tok, n_heads, hw = q_ref.shape[1:]
    rows = n_tok * n_heads
    flat = lambda ref: ref[...].reshape(-1, hw)

    @pl.when(g == 0)
    def _():
        q = flat(q_ref)
        lane_hi = lax.broadcasted_iota(jnp.int32, (1, hw), 1) >= HEAD_DIM
        w_ref[0:rows, :] = jnp.where(lane_hi, 0.0, q).astype(BF16)
        w_ref[rows:, :] = jnp.where(lane_hi, q, 0.0).astype(BF16)
        m_ref[...] = jnp.full_like(m_ref, NEG_INF)
        l_ref[...] = jnp.zeros_like(l_ref)
        acc_ref[...] = jnp.zeros_like(acc_ref)

    def absorb(scores, offsets, values):
        m_prev = m_ref[...]
        m_new = m_prev
        for s, off in zip(scores, offsets):
            m_new = jnp.maximum(m_new, jnp.max(s, axis=-1, keepdims=True) + off)
        alpha = jnp.exp2(m_prev - m_new)
        l = alpha * l_ref[...]
        acc = alpha * acc_ref[...]
        for s, off, v in zip(scores, offsets, values):
            p = jnp.exp2(s - (m_new - off))
            l = l + jnp.sum(p, axis=-1, keepdims=True)
            acc = acc + jnp.dot(p.astype(BF16), v, preferred_element_type=F32)
        l_ref[...] = l
        acc_ref[...] = acc
        m_ref[...] = m_new

    slope = slope_ref[...]
    w = w_ref[...]
    absorb([_nt(w, flat(k_refs[i]).astype(BF16)) + bp_ref[...] for i in range(group)],
           [slope * ((g * group + i + 1 - n_pages) * PAGE_SIZE).astype(F32) for i in range(group)],
           [flat(v_refs[i]).astype(BF16) for i in range(group)])

    @pl.when(g == pl.num_programs(1) - 1)
    def _():
        pad = jnp.zeros((LANES - rows, hw), F32)
        kn = jnp.concatenate([flat(kn_ref), pad], axis=0).astype(BF16)
        vn = jnp.concatenate([flat(vn_ref), pad], axis=0).astype(BF16)
        absorb([_nt(w, kn) + bn_ref[...]], [0.0], [vn])
        lm = lam_ref[...]
        lam = (jnp.exp(jnp.sum(lm[0:1] * lm[1:2], axis=-1, keepdims=True))
               - jnp.exp(jnp.sum(lm[2:3] * lm[3:4], axis=-1, keepdims=True)) + lam_init)
        o = acc_ref[...] * (1.0 / l_ref[...])
        o = o[0:rows] - lam * o[rows:]
        o = o * _rms_rows(o) * sub_ref[...] * (1.0 - lam_init)
        o_ref[0] = o.reshape(n_tok, n_heads, hw)


def _diff_sample(q, k_new, v_new, cache_k, cache_v, page_table, lam, subln, layer_j, *, lam_init, group):
    n, n_tok, n_heads, hw = q.shape
    n_pages = page_table.shape[1]
    assert n_pages % group == 0
    rows = 2 * n_tok * n_heads
    tab_page, tab_new, slope = _diff_sample_tables(n_tok, n_heads)
    fix = lambda i, g, pt: (0, 0)
    tok = pl.BlockSpec((1, n_tok, n_heads, hw), lambda i, g, pt: (i, 0, 0, 0))

    def page_spec(slot):
        return pl.BlockSpec((1, 1, PAGE_SIZE, n_heads, hw),
                            lambda i, g, pt: (layer_j, pt[i, g * group + slot], 0, 0, 0))

    pages = [page_spec(s) for s in range(group)]
    return pl.pallas_call(
        functools.partial(_diff_sample_kernel, lam_init=lam_init, group=group, n_pages=n_pages),
        out_shape=jax.ShapeDtypeStruct((n, n_tok, n_heads, hw), F32),
        grid_spec=pltpu.PrefetchScalarGridSpec(
            num_scalar_prefetch=1,
            grid=(n, n_pages // group),
            in_specs=[pl.BlockSpec(lam.shape, fix), pl.BlockSpec((1, hw), fix), pl.BlockSpec(tab_page.shape, fix),
                      pl.BlockSpec(tab_new.shape, fix), pl.BlockSpec(slope.shape, fix), tok, tok, tok]
                     + pages + pages,
            out_specs=tok,
            scratch_shapes=[pltpu.VMEM((rows, hw), BF16), pltpu.VMEM((rows, 1), F32),
                            pltpu.VMEM((rows, 1), F32), pltpu.VMEM((rows, hw), F32)],
        ),
        compiler_params=_params("parallel", "arbitrary"),
        name="diff_sample",
    )(page_table, lam, subln.reshape(1, hw), tab_page, tab_new, slope, q, k_new, v_new,
      *([cache_k] * group), *([cache_v] * group))


def _by_head(a, *, shp):
    n_tok = shp[1]
    a = a.reshape(shp).transpose(0, 2, 1, 3)
    return jnp.pad(a, ((0, 0), (0, 0), (0, -n_tok % 8), (0, 0)))


def _row_tile(rows, target):
    tile = min(rows, target)
    while rows % tile:
        tile //= 2
    return tile


def kernel(x_prompt, x_sample, state_conv, cache_b_k, cache_b_v, cache_c_k, cache_c_v, page_table, norm_gain,
           ffn_w_gate, ffn_w_up, ffn_w_down, w_in_even, conv_w, qk_gain_b, w_out_even, w_in_odd, qk_gain_c,
           lambda_c, subln_c, w_out_odd):
    batch, seq, d = x_prompt.shape
    n, n_tok, _ = x_sample.shape
    depth = norm_gain.shape[0]
    w_buf = cache_b_k.shape[2]
    xp = x_prompt.reshape(batch * seq, d)
    xs = x_sample.reshape(n * n_tok, d)
    tp_ffn, tp, ts = _row_tile(batch * seq, 1024), _row_tile(batch * seq, 512), _row_tile(n * n_tok, 512)
    tq = _row_tile(seq, 512)
    wg, wu, wd = ffn_w_gate.astype(BF16), ffn_w_up.astype(BF16), ffn_w_down.astype(BF16)
    conv_p, conv_s, bk_p, bv_p, bk_s, bv_s, ck_p, cv_p, ck_s, cv_s = ([] for _ in range(10))

    for layer in range(depth):
        g = norm_gain[layer]
        j = layer // 2
        xp = _half_ffn(xp, g[0], wg, wu, wd, (layer, 0), tm=tp_ffn)
        xs = _half_ffn(xs, g[0], wg, wu, wd, (layer, 0), tm=ts)
        if layer % 2 == 0:
            w_in, w_out = w_in_even[j].astype(BF16), w_out_even[j].astype(BF16)
            cw = w_in.shape[1] // 6
            heads = (cw // HEAD_DIM, HEAD_DIM)
            bg, u, kf, vf, qs, ks, vs = _inproj_even(xp, g[1], w_in, qk_gain_b[j], tm=tp, dilations=DILATIONS)
            res = [_dilated_prompt(qs[b], ks[b], vs[b], batch=batch, dilation=dil) for b, dil in enumerate(DILATIONS)]
            xp = _even_out(xp, bg, u, None, [r[0] for r in res], [r[1] for r in res], conv_w[j], w_out,
                           tm=tp, seq_rows=seq, sample=False)
            conv_p.append(u.reshape(batch, seq, cw)[:, seq - (CONV_K - 1):])
            last = lambda a: a.reshape(batch, seq, cw)[:, seq - w_buf:].reshape(batch, w_buf, *heads)
            bk_p.append(last(kf))
            bv_p.append(last(vf))

            bg, u, kf, vf, (qb,), _, _ = _inproj_even(xs, g[1], w_in, qk_gain_b[j], tm=ts)
            shp = (n, n_tok) + heads
            by_head = functools.partial(_by_head, shp=shp)
            o = _dilated_sample(by_head(qb.astype(F32)), by_head(kf), by_head(vf),
                                cache_b_k.transpose(0, 1, 3, 4, 2), cache_b_v.transpose(0, 1, 3, 4, 2), j,
                                n_tok=n_tok)
            o = o[:, :, :n_tok].transpose(0, 2, 1, 3)
            st = state_conv[j]
            hist2 = jnp.pad(st, ((0, 0), (0, n_tok - (CONV_K - 1)), (0, 0))).reshape(n * n_tok, cw)
            hist1 = jnp.pad(st[:, 1:], ((0, 0), (0, n_tok - 1), (0, 0))).reshape(n * n_tok, cw)
            xs = _even_out(xs, bg, u, (hist1, hist2), [o.reshape(n * n_tok, cw)], [], conv_w[j], w_out,
                           tm=ts, seq_rows=n_tok, sample=True)
            conv_s.append(u.reshape(n, n_tok, cw)[:, n_tok - (CONV_K - 1):])
            bk_s.append(kf.reshape(shp))
            bv_s.append(vf.reshape(shp))
        else:
            w_in, w_out = w_in_odd[j].astype(BF16), w_out_odd[j].astype(BF16)
            cw = w_in.shape[1] // 3
            heads = (cw // (2 * HEAD_DIM), 2 * HEAD_DIM)
            lam_init = 0.8 - 0.6 * math.exp(-0.3 * layer)
            qb, ka, kb, kf, vf, vt = _inproj_odd(xp, g[1], w_in, qk_gain_c[j], tm=tq, kv_block=tq)
            o = _diff_prompt(qb, ka, kb, vt, lambda_c[j], subln_c[j], batch=batch, lam_init=lam_init, tq=tq)
            xp = _outproj(xp, o, w_out, tm=tp)
            ck_p.append(kf.reshape(batch, seq // PAGE_SIZE, PAGE_SIZE, *heads))
            cv_p.append(vf.reshape(batch, seq // PAGE_SIZE, PAGE_SIZE, *heads))

            qb, kf, vf = _inproj_odd(xs, g[1], w_in, qk_gain_c[j], tm=ts)
            shp = (n, n_tok) + heads
            o = _diff_sample(qb.astype(F32).reshape(shp), kf.reshape(shp), vf.reshape(shp), cache_c_k, cache_c_v,
                             page_table, lambda_c[j], subln_c[j], j, lam_init=lam_init,
                             group=math.gcd(page_table.shape[1], 8))
            xs = _outproj(xs, o.reshape(n * n_tok, cw), w_out, tm=ts)
            ck_s.append(kf.reshape(shp))
            cv_s.append(vf.reshape(shp))
        xp = _half_ffn(xp, g[2], wg, wu, wd, (layer, 1), tm=tp_ffn)
        xs = _half_ffn(xs, g[2], wg, wu, wd, (layer, 1), tm=ts)

    return (xp.reshape(batch, seq, d), xs.reshape(n, n_tok, d),
            jnp.stack(conv_p), jnp.stack(conv_s),
            jnp.stack(bk_p), jnp.stack(bv_p), jnp.stack(bk_s), jnp.stack(bv_s),
            jnp.stack(ck_p), jnp.stack(cv_p), jnp.stack(ck_s), jnp.stack(cv_s))
```

```python
import functools
import math

import jax
import jax.numpy as jnp
from jax import lax
from jax.experimental import pallas as pl
from jax.experimental.pallas import tpu as pltpu

F32 = jnp.float32
BF16 = jnp.bfloat16

HEAD_DIM = 64
CONV_K = 3
SUB_WINDOW = 128
DILATIONS = (1, 4, 16)
PAGE_SIZE = 128
ATTN_SCALE = HEAD_DIM ** -0.5
LOG2E = math.log2(math.e)
RMS_EPS = 1e-6
NEG_INF = -1e30

V7X_VMEM_LIMIT_BYTES = 56 * 1024 * 1024
MXU_COLS = 256
LANES = 128


def _params(*sem):
    return pltpu.CompilerParams(dimension_semantics=sem, vmem_limit_bytes=V7X_VMEM_LIMIT_BYTES)


def _rms_rows(x):
    return lax.rsqrt(jnp.mean(x * x, axis=-1, keepdims=True) + RMS_EPS)


def _nt(a, b):
    return lax.dot_general(a, b, (((1,), (1,)), ((), ())), preferred_element_type=F32)


def _group_ones(width, group):
    i = jnp.arange(width) // group
    return (i[:, None] == i[None, :]).astype(BF16)


def _group_rmsnorm(x, bd_ref, gain):
    x2 = x * x
    hi = x2.astype(BF16)
    lo = (x2 - hi.astype(F32)).astype(BF16)
    bd = bd_ref[...]
    cols = []
    for c in range(x.shape[1] // MXU_COLS):
        sl = slice(c * MXU_COLS, (c + 1) * MXU_COLS)
        cols.append(jnp.dot(hi[:, sl], bd, preferred_element_type=F32)
                    + jnp.dot(lo[:, sl], bd, preferred_element_type=F32))
    ss = jnp.concatenate(cols, axis=1)
    return x * lax.rsqrt(ss * (1.0 / HEAD_DIM) + RMS_EPS) * gain


def _ffn_kernel(x_ref, g_ref, wg_ref, wu_ref, wd_ref, o_ref, *, tf):
    x = x_ref[...]
    h = (x * _rms_rows(x) * g_ref[...]).astype(BF16)
    acc = jnp.zeros(x.shape, F32)
    for c in range(wg_ref.shape[1] // tf):
        cols = slice(c * tf, (c + 1) * tf)
        a = jnp.dot(h, wg_ref[:, cols], preferred_element_type=F32)
        b = jnp.dot(h, wu_ref[:, cols], preferred_element_type=F32)
        z = (a * (1.0 / (1.0 + jnp.exp(-a))) * b).astype(BF16)
        acc = acc + jnp.dot(z, wd_ref[cols, :], preferred_element_type=F32)
    o_ref[...] = x + 0.5 * acc


def _half_ffn(x, g, wg, wu, wd, which, *, tm, tf=MXU_COLS):
    t, d = x.shape
    f = wg.shape[-1]
    return pl.pallas_call(
        functools.partial(_ffn_kernel, tf=tf),
        out_shape=jax.ShapeDtypeStruct((t, d), F32),
        grid=(t // tm,),
        in_specs=[
            pl.BlockSpec((tm, d), lambda i: (i, 0)),
            pl.BlockSpec((1, d), lambda i: (0, 0)),
            pl.BlockSpec((None, None, d, f), lambda i: (*which, 0, 0)),
            pl.BlockSpec((None, None, d, f), lambda i: (*which, 0, 0)),
            pl.BlockSpec((None, None, f, d), lambda i: (*which, 0, 0)),
        ],
        out_specs=pl.BlockSpec((tm, d), lambda i: (i, 0)),
        compiler_params=_params("parallel"),
        name="half_ffn",
    )(x, g.reshape(1, d), wg, wu, wd)


def _inproj_even_kernel(x_ref, g_ref, w_ref, bd_ref, gq_ref, gk_ref, bg_ref, u_ref, kf_ref, vf_ref, *rest, dilations):
    n_d = len(dilations)
    stages = rest[3 * n_d:]
    x = x_ref[...]
    h = (x * _rms_rows(x) * g_ref[...]).astype(BF16)
    tm, cw = u_ref.shape

    def col(c):
        return jnp.dot(h, w_ref[:, c * cw:(c + 1) * cw], preferred_element_type=F32)

    bg_ref[...] = col(0)
    u_ref[...] = col(1) * col(2)
    q = _group_rmsnorm(col(3), bd_ref, gq_ref[...])
    k = _group_rmsnorm(col(4), bd_ref, gk_ref[...])
    v = col(5)
    kf_ref[...] = k
    vf_ref[...] = v
    for n, val in enumerate((q * ATTN_SCALE, k, v)):
        stage = stages[n]
        for c in range(cw // LANES):
            stage[c] = val[:, c * LANES:(c + 1) * LANES]
        for d, o_ref in zip(dilations, rest[n * n_d:(n + 1) * n_d]):
            for r in range(d):
                for c in range(cw // LANES):
                    lanes = slice(r * cw + c * LANES, r * cw + (c + 1) * LANES)
                    o_ref[:, lanes] = stage[c, pl.ds(r, tm // d, stride=d), :].astype(BF16)


def _inproj_even(x, g, w, qk_g, *, tm, dilations=(1,)):
    t, d_model = x.shape
    cw = w.shape[1] // 6
    reps = cw // HEAD_DIM
    row = lambda i: (i, 0)
    fix = lambda i: (0, 0)
    n_d = len(dilations)
    outs = [jax.ShapeDtypeStruct((t, cw), F32)] * 4 \
        + [jax.ShapeDtypeStruct((t // d, d * cw), BF16) for d in dilations] * 3
    out_specs = [pl.BlockSpec((tm, cw), row)] * 4 + [pl.BlockSpec((tm // d, d * cw), row) for d in dilations] * 3
    res = pl.pallas_call(
        functools.partial(_inproj_even_kernel, dilations=dilations),
        out_shape=outs,
        grid=(t // tm,),
        in_specs=[
            pl.BlockSpec((tm, d_model), row),
            pl.BlockSpec((1, d_model), fix),
            pl.BlockSpec(w.shape, fix),
            pl.BlockSpec((MXU_COLS, MXU_COLS), fix),
            pl.BlockSpec((1, cw), fix),
            pl.BlockSpec((1, cw), fix),
        ],
        out_specs=out_specs,
        scratch_shapes=[pltpu.VMEM((cw // LANES, tm, LANES), F32)] * 3,
        compiler_params=_params("parallel"),
        name="inproj_even",
    )(x, g.reshape(1, d_model), w, _group_ones(MXU_COLS, HEAD_DIM),
      jnp.tile(qk_g[0], reps).reshape(1, cw), jnp.tile(qk_g[1], reps).reshape(1, cw))
    return list(res[:4]) + [list(res[4 + n * n_d:4 + (n + 1) * n_d]) for n in range(3)]


def _inproj_odd_kernel(x_ref, g_ref, w_ref, bd_ref, gq_ref, gk_ref, *rest, kv_block):
    if kv_block:
        bias_ref, qb_ref, ka_ref, kb_ref, kf_ref, vf_ref, vt_ref = rest
    else:
        qb_ref, kf_ref, vf_ref = rest
    x = x_ref[...]
    h = (x * _rms_rows(x) * g_ref[...]).astype(BF16)
    cw = qb_ref.shape[1]

    def col(c):
        return jnp.dot(h, w_ref[:, c * cw:(c + 1) * cw], preferred_element_type=F32)

    q = _group_rmsnorm(col(0), bd_ref, gq_ref[...])
    k = _group_rmsnorm(col(1), bd_ref, gk_ref[...])
    qb_ref[...] = (q * (ATTN_SCALE * LOG2E)).astype(BF16)
    kf_ref[...] = k
    v = col(2)
    vf_ref[...] = v
    if kv_block:
        first_half = lax.broadcasted_iota(jnp.int32, (1, LANES), 1) < HEAD_DIM
        for hh in range(cw // LANES):
            sl = slice(hh * LANES, (hh + 1) * LANES)
            k_h, tab = k[:, sl], bias_ref[hh]
            ka_ref[:, sl] = jnp.where(first_half, k_h, tab).astype(BF16)
            kb_ref[:, sl] = jnp.where(first_half, pltpu.roll(k_h, HEAD_DIM, 1), tab).astype(BF16)
        vt = v.T.astype(BF16)
        for c in range(vt_ref.shape[0]):
            vt_ref[c] = vt[:, c * kv_block:(c + 1) * kv_block]


BIAS_LANES = 3


def _key_bias_table(n_heads, block):
    slope = jnp.asarray([_alibi_slope(h, n_heads) * LOG2E for h in range(n_heads)], F32)
    rest = slope[:, None] * jnp.arange(block, dtype=F32)[None, :]
    tab = jnp.zeros((n_heads, block, LANES), F32)
    for t in range(BIAS_LANES):
        bits = lax.bitcast_convert_type(rest, jnp.uint32) & jnp.uint32(0xFFFF0000)
        term = lax.bitcast_convert_type(bits, F32)
        tab = tab.at[:, :, HEAD_DIM + t].set(term)
        rest = rest - term
    return tab


def _inproj_odd(x, g, w, qk_g, *, tm, kv_block=0):
    t, d = x.shape
    cw = w.shape[1] // 3
    reps = cw // HEAD_DIM
    row = lambda i: (i, 0)
    fix = lambda i: (0, 0)
    ins = [x, g.reshape(1, d), w, _group_ones(MXU_COLS, HEAD_DIM),
           jnp.tile(qk_g[0], reps).reshape(1, cw), jnp.tile(qk_g[1], reps).reshape(1, cw)]
    in_specs = [pl.BlockSpec((tm, d), row), pl.BlockSpec((1, d), fix), pl.BlockSpec(w.shape, fix),
                pl.BlockSpec((MXU_COLS, MXU_COLS), fix), pl.BlockSpec((1, cw), fix), pl.BlockSpec((1, cw), fix)]
    tile = pl.BlockSpec((tm, cw), row)
    tok_bf16, tok_f32 = jax.ShapeDtypeStruct((t, cw), BF16), jax.ShapeDtypeStruct((t, cw), F32)
    if kv_block:
        assert tm == kv_block
        bias = _key_bias_table(cw // LANES, kv_block)
        ins.append(bias)
        in_specs.append(pl.BlockSpec(bias.shape, lambda i: (0, 0, 0)))
        outs = [tok_bf16, tok_bf16, tok_bf16, tok_f32, tok_f32,
                jax.ShapeDtypeStruct((t // kv_block, cw, kv_block), BF16)]
        out_specs = [tile] * 5 + [pl.BlockSpec((tm // kv_block, cw, kv_block), lambda i: (i, 0, 0))]
    else:
        outs = [tok_bf16, tok_f32, tok_f32]
        out_specs = [tile] * 3
    return pl.pallas_call(
        functools.partial(_inproj_odd_kernel, kv_block=kv_block),
        out_shape=outs,
        grid=(t // tm,),
        in_specs=in_specs,
        out_specs=out_specs,
        compiler_params=_params("parallel"),
        name="inproj_odd",
    )(*ins)


def _outproj_kernel(x_ref, a_ref, w_ref, o_ref):
    o_ref[...] = x_ref[...] + jnp.dot(a_ref[...].astype(BF16), w_ref[...], preferred_element_type=F32)


def _outproj(x, a, w, *, tm):
    t, d = x.shape
    row = lambda i: (i, 0)
    return pl.pallas_call(
        _outproj_kernel,
        out_shape=jax.ShapeDtypeStruct((t, d), F32),
        grid=(t // tm,),
        in_specs=[pl.BlockSpec((tm, d), row), pl.BlockSpec((tm, a.shape[1]), row),
                  pl.BlockSpec(w.shape, lambda i: (0, 0))],
        out_specs=pl.BlockSpec((tm, d), row),
        compiler_params=_params("parallel"),
        name="outproj",
    )(x, a, w)


def _alibi_slope(h, n_heads):
    return 2.0 ** (-8.0 * (h + 1) / n_heads)


def _dilated_prompt_kernel(bias_ref, q_ref, kc_ref, kp_ref, vc_ref, vp_ref, o_ref, lse_ref, *, n_heads):
    step = pl.program_id(2)
    tq = kp_ref.shape[1]
    n_sub = q_ref.shape[1] // tq
    lane_hi = lax.broadcasted_iota(jnp.int32, (1, LANES), 1) >= HEAD_DIM
    for sub in range(n_sub):
        rows = slice(sub * tq, (sub + 1) * tq)
        before = slice((sub - 1) * tq, sub * tq)
        variant = jnp.minimum(step, 1) if sub == 0 else 1
        for p in range(n_heads // 2):
            sl = slice(p * LANES, (p + 1) * LANES)
            q = q_ref[0, rows, sl]
            k_prev, v_prev = (kp_ref[0, :, sl], vp_ref[0, :, sl]) if sub == 0 else (kc_ref[0, before, sl],
                                                                                   vc_ref[0, before, sl])
            k = jnp.concatenate([k_prev, kc_ref[0, rows, sl]], axis=0)
            v = jnp.concatenate([v_prev, vc_ref[0, rows, sl]], axis=0)
            o_pair = jnp.zeros((tq, LANES), F32)
            lse_pair = jnp.zeros((tq, LANES), F32)
            for e in range(2):
                keep = lane_hi if e else jnp.logical_not(lane_hi)
                qe = jnp.where(keep, q, jnp.zeros_like(q))
                ve = jnp.where(keep, v, jnp.zeros_like(v))
                s = _nt(qe, k) + bias_ref[variant, 2 * p + e]
                m = jnp.max(s, axis=-1, keepdims=True)
                pr = jnp.exp(s - m)
                l = jnp.sum(pr, axis=-1, keepdims=True)
                o_pair = o_pair + jnp.dot(pr.astype(BF16), ve, preferred_element_type=F32) * (1.0 / l)
                lse_pair = jnp.where(keep, m + jnp.log(l), lse_pair)
            o_ref[0, rows, sl] = o_pair
            lse_ref[0, rows, sl] = lse_pair


def _dilated_bias_table(n_heads, dilation):
    tq = SUB_WINDOW
    qi = jnp.arange(tq)[:, None]
    kj = jnp.arange(2 * tq)[None, :]
    dist = qi + tq - kj
    band = (dist >= 0) & (dist <= SUB_WINDOW)
    slope = jnp.asarray([_alibi_slope(h, n_heads) for h in range(n_heads)], F32)[:, None, None]
    bias = -slope * (dilation * dist).astype(F32)[None]
    return jnp.stack([jnp.where(band & (kj >= tq), bias, NEG_INF), jnp.where(band, bias, NEG_INF)])


def _dilated_prompt(q, k, v, *, batch, dilation):
    d = dilation
    t, width = q.shape[0] * d, q.shape[1] // d
    s_len = t // batch
    tq = SUB_WINDOW
    n_sub = 2 if s_len % (2 * d * tq) == 0 else 1
    steps = s_len // (d * tq * n_sub)
    assert steps * d * tq * n_sub == s_len
    n_heads = width // HEAD_DIM
    bias = _dilated_bias_table(n_heads, d)
    view = lambda a: a.reshape(batch, s_len // d, d * width)
    cur = lambda b, r, i: (b, i, r)
    prev = lambda b, r, i: (b, jnp.maximum(i * n_sub - 1, 0), r)
    blk, halo = (1, n_sub * tq, width), (1, tq, width)
    o, lse = pl.pallas_call(
        functools.partial(_dilated_prompt_kernel, n_heads=n_heads),
        out_shape=[jax.ShapeDtypeStruct((batch, s_len // d, d * width), F32)] * 2,
        grid=(batch, d, steps),
        in_specs=[pl.BlockSpec(bias.shape, lambda b, r, i: (0, 0, 0, 0)),
                  pl.BlockSpec(blk, cur), pl.BlockSpec(blk, cur), pl.BlockSpec(halo, prev),
                  pl.BlockSpec(blk, cur), pl.BlockSpec(halo, prev)],
        out_specs=[pl.BlockSpec(blk, cur)] * 2,
        compiler_params=_params("parallel", "parallel", "arbitrary"),
        name=f"dilated_prompt_d{d}",
    )(bias, view(q), view(k), view(k), view(v), view(v))
    return o.reshape(t // d, d * width), lse.reshape(t // d, d * width)


def _even_out_kernel(x_ref, bg_ref, u_ref, *rest, seq_rows, sample):
    if sample:
        s1_ref, s2_ref = rest[:2]
        rest = rest[2:]
    else:
        halo_ref = rest[0]
        rest = rest[1:]
    tm, cw = u_ref.shape
    if sample:
        o1_ref, cw_ref, w_ref, out_ref = rest
    else:
        n_d = len(DILATIONS)
        branch_refs, (cw_ref, w_ref, out_ref), stages = rest[:2 * n_d], rest[2 * n_d:2 * n_d + 3], rest[2 * n_d + 3:]

        def token_rows(n):
            d, g_ref = DILATIONS[n % n_d], branch_refs[n]
            if d == 1:
                return g_ref[...]
            stage = stages[n]
            for r in range(d):
                for c in range(cw // LANES):
                    lanes = slice(r * cw + c * LANES, r * cw + (c + 1) * LANES)
                    stage[c, pl.ds(r, tm // d, stride=d), :] = g_ref[:, lanes]
            return jnp.concatenate([stage[c] for c in range(cw // LANES)], axis=1)
    u = u_ref[...]
    r1 = pltpu.roll(u, 1, 0)
    r2 = pltpu.roll(u, 2, 0)
    if sample:
        t_in_seq = lax.broadcasted_iota(jnp.int32, (tm, cw), 0) % seq_rows
        u1 = jnp.where(t_in_seq >= 1, r1, s1_ref[...])
        u2 = jnp.where(t_in_seq >= 2, r2, s2_ref[...])
    else:
        first = (pl.program_id(0) % (seq_rows // tm)) == 0
        halo = jnp.where(first, 0.0, halo_ref[...])
        row8 = lax.broadcasted_iota(jnp.int32, halo.shape, 0)
        top1 = jnp.where(row8 < 1, pltpu.roll(halo, 1, 0), r1[0:8])
        top2 = jnp.where(row8 < 2, pltpu.roll(halo, 2, 0), r2[0:8])
        u1 = jnp.concatenate([top1, r1[8:]], axis=0)
        u2 = jnp.concatenate([top2, r2[8:]], axis=0)
    cwt = cw_ref[...]
    a = bg_ref[...] * (cwt[0:1] * u2 + cwt[1:2] * u1 + cwt[2:3] * u)
    if sample:
        b = o1_ref[...]
    else:
        o1, o2, o3, l1, l2, l3 = (token_rows(n) for n in range(2 * n_d))
        mx = jnp.maximum(jnp.maximum(l1, l2), l3)
        e1, e2, e3 = jnp.exp(l1 - mx), jnp.exp(l2 - mx), jnp.exp(l3 - mx)
        b = (e1 * o1 + e2 * o2 + e3 * o3) * (1.0 / (e1 + e2 + e3))
    out_ref[...] = (x_ref[...]
                    + jnp.dot(a.astype(BF16), w_ref[0:cw, :], preferred_element_type=F32)
                    + jnp.dot(b.astype(BF16), w_ref[cw:, :], preferred_element_type=F32))


def _even_out(x, bg, u, hist, outs, lses, conv_w, w_out, *, tm, seq_rows, sample):
    t, d = x.shape
    cw = u.shape[1]
    row = lambda i: (i, 0)
    fix = lambda i: (0, 0)
    tile = pl.BlockSpec((tm, cw), row)
    if sample:
        hist_in, hist_specs = list(hist), [tile, tile]
        branch_specs, scratch = [tile], []
    else:
        hist_in = [u]
        hist_specs = [pl.BlockSpec((8, cw), lambda i: (jnp.maximum(i * (tm // 8) - 1, 0), 0))]
        branch_specs = [pl.BlockSpec((tm // dil, dil * cw), row) for dil in DILATIONS] * 2
        scratch = [pltpu.VMEM((cw // LANES, tm, LANES), F32)] * (2 * len(DILATIONS))
    return pl.pallas_call(
        functools.partial(_even_out_kernel, seq_rows=seq_rows, sample=sample),
        out_shape=jax.ShapeDtypeStruct((t, d), F32),
        grid=(t // tm,),
        in_specs=[pl.BlockSpec((tm, d), row), tile, tile] + hist_specs + branch_specs
                 + [pl.BlockSpec(conv_w.shape, fix), pl.BlockSpec(w_out.shape, fix)],
        out_specs=pl.BlockSpec((tm, d), row),
        scratch_shapes=scratch,
        compiler_params=_params("parallel"),
        name="even_out",
    )(x, bg, u, *hist_in, *outs, *lses, conv_w, w_out)


def _diff_prompt_kernel(lam_ref, sub_ref, slope_ref, q_ref, ka_ref, kb_ref, vt_ref, o_ref,
                        qs_ref, sa_ref, sb_ref, ma_ref, mb_ref, m_ref, l_ref, acc_ref, *, lam_init):
    i = pl.program_id(2)
    tq = q_ref.shape[1]
    n_grp = 2 * tq // MXU_COLS
    q = q_ref[0].astype(F32)
    lane = lax.broadcasted_iota(jnp.int32, (1, LANES), 1)
    ones = jnp.where((lane >= HEAD_DIM) & (lane < HEAD_DIM + BIAS_LANES), 1.0, 0.0)
    qs_ref[0:tq, :] = jnp.where(lane < HEAD_DIM, q, ones).astype(BF16)
    qs_ref[tq:, :] = jnp.where(lane < HEAD_DIM, pltpu.roll(q, HEAD_DIM, 1), ones).astype(BF16)
    m_ref[...] = jnp.full_like(m_ref, NEG_INF)
    l_ref[...] = jnp.zeros_like(l_ref)
    acc_ref[...] = jnp.zeros_like(acc_ref)
    slope = slope_ref[0][:, 0:1]
    key_row = lax.broadcasted_iota(jnp.int32, (tq, LANES), 0)
    q_col = lax.broadcasted_iota(jnp.int32, (tq, LANES), 1)

    buf_a, buf_b = (sa_ref, ma_ref), (sb_ref, mb_ref)

    def score(j, buf):
        s_ref, smax_ref = buf
        rows = pl.ds(pl.multiple_of(j * tq, tq), tq)
        for c in range(n_grp):
            cols = slice(c * MXU_COLS, (c + 1) * MXU_COLS)
            k_ref = ka_ref if c < n_grp // 2 else kb_ref
            s = _nt(k_ref[0, rows, :], qs_ref[cols, :])
            s_ref[:, cols] = s
            smax_ref[:, cols] = jnp.max(s, axis=0, keepdims=True)

    def absorb(j, buf, diagonal):
        s_ref, smax_ref = buf
        vt = vt_ref[j]
        off = slope * ((j - i) * tq).astype(F32)
        m_all, l_all = m_ref[...], l_ref[...]
        m_out, l_out = [], []
        for c in range(n_grp):
            cols = slice(c * MXU_COLS, (c + 1) * MXU_COLS)
            s = s_ref[:, cols]
            if diagonal:
                s = jnp.concatenate(
                    [jnp.where(key_row <= q_col + (c * MXU_COLS + cc * LANES) % tq,
                               s[:, cc * LANES:(cc + 1) * LANES], NEG_INF) for cc in range(MXU_COLS // LANES)], axis=1)
                smax = jnp.max(s, axis=0, keepdims=True)
            else:
                smax = smax_ref[:, cols]
            m_prev = m_all[:, cols]
            m_new = jnp.maximum(m_prev, smax + off)
            alpha = jnp.exp2(m_prev - m_new)
            p = jnp.exp2(s - (m_new - off))
            l_out.append(alpha * l_all[:, cols] + jnp.sum(p, axis=0, keepdims=True))
            acc_ref[:, cols] = alpha * acc_ref[:, cols] + jnp.dot(vt, p.astype(BF16), preferred_element_type=F32)
            m_out.append(m_new)
        m_ref[...] = jnp.concatenate(m_out, axis=1)
        l_ref[...] = jnp.concatenate(l_out, axis=1)

    score(0, buf_a)

    def pair(jj, carry):
        j = 2 * jj
        score(j + 1, buf_b)
        absorb(j, buf_a, False)
        score(j + 2, buf_a)
        absorb(j + 1, buf_b, False)
        return carry

    lax.fori_loop(0, i // 2, pair, 0)

    @pl.when(i % 2 == 1)
    def _():
        score(i, buf_b)
        absorb(i - 1, buf_a, False)
        absorb(i, buf_b, True)

    @pl.when(i % 2 == 0)
    def _():
        absorb(i, buf_a, True)

    lm = lam_ref[...]
    lam = (jnp.exp(jnp.sum(lm[0:1] * lm[1:2], axis=-1, keepdims=True))
           - jnp.exp(jnp.sum(lm[2:3] * lm[3:4], axis=-1, keepdims=True)) + lam_init)
    acc = acc_ref[...]
    inv = 1.0 / l_ref[...]
    o_t = acc[:, :tq] * inv[:, :tq] - lam * (acc[:, tq:] * inv[:, tq:])
    o = o_t.T
    o = o * _rms_rows(o) * sub_ref[...] * (1.0 - lam_init)
    o_ref[0] = o.astype(BF16)


def _diff_prompt(q, ka, kb, vt, lam, subln, *, batch, lam_init, tq):
    t, width = q.shape
    s_len = t // batch
    hw = 2 * HEAD_DIM
    n_heads = width // hw
    nq = s_len // tq
    assert vt.shape == (t // tq, width, tq) and nq * tq == s_len
    slopes = jnp.asarray([_alibi_slope(h, n_heads) * LOG2E for h in range(n_heads)], F32)
    slope_tab = jnp.broadcast_to(slopes[:, None, None], (n_heads, 1, LANES))
    out = pl.pallas_call(
        functools.partial(_diff_prompt_kernel, lam_init=lam_init),
        out_shape=jax.ShapeDtypeStruct((batch, s_len, width), BF16),
        grid=(batch, n_heads, nq),
        in_specs=[
            pl.BlockSpec(lam.shape, lambda b, h, i: (0, 0)),
            pl.BlockSpec((1, hw), lambda b, h, i: (0, 0)),
            pl.BlockSpec((1, 1, LANES), lambda b, h, i: (h, 0, 0)),
            pl.BlockSpec((1, tq, hw), lambda b, h, i: (b, i, h)),
            pl.BlockSpec((1, s_len, hw), lambda b, h, i: (b, 0, h)),
            pl.BlockSpec((1, s_len, hw), lambda b, h, i: (b, 0, h)),
            pl.BlockSpec((nq, hw, tq), lambda b, h, i: (b, h, 0)),
        ],
        out_specs=pl.BlockSpec((1, tq, hw), lambda b, h, i: (b, i, h)),
        scratch_shapes=[pltpu.VMEM((2 * tq, hw), BF16), pltpu.VMEM((tq, 2 * tq), F32), pltpu.VMEM((tq, 2 * tq), F32)]
                       + [pltpu.VMEM((1, 2 * tq), F32)] * 4 + [pltpu.VMEM((hw, 2 * tq), F32)],
        compiler_params=_params("parallel", "parallel", "arbitrary"),
        name="diff_prompt",
    )(lam, subln.reshape(1, hw), slope_tab, *(a.reshape(batch, s_len, width) for a in (q, ka, kb)), vt)
    return out.reshape(t, width)


SAMPLE_ROWS = 16


def _dilated_sample_tables(n_tok, w_buf):
    def tables(dist, valid):
        count = sum(((dist % d == 0) & (dist <= d * SUB_WINDOW)).astype(F32) for d in DILATIONS)
        held = valid & (count > 0)
        return (jnp.where(held, dist.astype(F32), -NEG_INF),
                jnp.where(held, jnp.log(jnp.maximum(count, 1.0)), 0.0))

    t = jnp.arange(SAMPLE_ROWS)[:, None]
    j = jnp.arange(LANES)[None, :]
    cache = tables(w_buf + t - jnp.arange(w_buf)[None, :], t < n_tok)
    new = tables(t - j, (t < n_tok) & (j <= t))
    return cache + new


def _dilated_sample_kernel(dc_ref, lc_ref, dn_ref, ln_ref, q_ref, kn_ref, vn_ref, kt_ref, vt_ref, o_ref):
    n_heads, n_tok, hd = q_ref.shape[1:]
    pad_q = jnp.zeros((SAMPLE_ROWS - n_tok, hd), F32)
    pad_n = jnp.zeros((LANES - n_tok, hd), F32)
    rmax = lambda a: jnp.max(a, axis=-1, keepdims=True)
    rsum = lambda a: jnp.sum(a, axis=-1, keepdims=True)
    for h in range(n_heads):
        slope = _alibi_slope(h, n_heads)
        q = jnp.concatenate([q_ref[0, h], pad_q], axis=0).astype(BF16)
        kn = jnp.concatenate([kn_ref[0, h], pad_n], axis=0).astype(BF16)
        vn = jnp.concatenate([vn_ref[0, h], pad_n], axis=0).astype(BF16)
        s = jnp.dot(q, kt_ref[0, 0, h].astype(BF16), preferred_element_type=F32)
        a = s - slope * dc_ref[...] + lc_ref[...]
        an = _nt(q, kn) - slope * dn_ref[...] + ln_ref[...]
        m = jnp.maximum(rmax(a), rmax(an))
        e = jnp.exp(a - m)
        en = jnp.exp(an - m)
        o = _nt(e.astype(BF16), vt_ref[0, 0, h].astype(BF16)) + jnp.dot(en.astype(BF16), vn,
                                                                        preferred_element_type=F32)
        o_ref[0, h] = (o * (1.0 / (rsum(e) + rsum(en))))[0:n_tok]


def _dilated_sample(q, k_new, v_new, cache_kt, cache_vt, layer_j, *, n_tok):
    n, n_heads, tok_pad, hd = q.shape
    w_buf = cache_kt.shape[-1]
    assert n_tok <= min(DILATIONS[1:]) and w_buf == max(DILATIONS) * SUB_WINDOW
    tabs = _dilated_sample_tables(n_tok, w_buf)
    fix = lambda i: (0, 0)
    tok = pl.BlockSpec((1, n_heads, tok_pad, hd), lambda i: (i, 0, 0, 0))
    blk = pl.BlockSpec((1, 1, n_heads, hd, w_buf), lambda i: (layer_j, i, 0, 0, 0))
    return pl.pallas_call(
        _dilated_sample_kernel,
        out_shape=jax.ShapeDtypeStruct((n, n_heads, tok_pad, hd), F32),
        grid=(n,),
        in_specs=[pl.BlockSpec(tb.shape, fix) for tb in tabs] + [tok, tok, tok, blk, blk],
        out_specs=tok,
        compiler_params=_params("parallel"),
        name="dilated_sample",
    )(*tabs, q, k_new, v_new, cache_kt, cache_vt)


def _diff_sample_tables(n_tok, n_heads):
    r = jnp.arange(2 * n_tok * n_heads)[:, None]
    t, h = (r // n_heads) % n_tok, r % n_heads
    slope = (jnp.exp2(-8.0 * (h + 1) / n_heads) * LOG2E).astype(F32)
    c = jnp.arange(PAGE_SIZE * n_heads)[None, :]
    tab_page = jnp.where(c % n_heads == h, slope * (c // n_heads - PAGE_SIZE).astype(F32), NEG_INF)
    cn = jnp.arange(LANES)[None, :]
    j = cn // n_heads
    tab_new = jnp.where((cn < n_tok * n_heads) & (cn % n_heads == h) & (j <= t), slope * j.astype(F32), NEG_INF)
    return tab_page, tab_new, slope


def _diff_sample_kernel(pt_ref, lam_ref, sub_ref, bp_ref, bn_ref, slope_ref, q_ref, kn_ref, vn_ref, *rest,
                        lam_init, group, n_pages):
    k_refs, v_refs = rest[:group], rest[group:2 * group]
    o_ref, w_ref, m_ref, l_ref, acc_ref = rest[2 * group:]
    g = pl.program_id(1)
    n_tok, n_heads, hw = q_ref.shape[1:]
    rows = n_tok * n_heads
    flat = lambda ref: ref[...].reshape(-1, hw)

    @pl.when(g == 0)
    def _():
        q = flat(q_ref)
        lane_hi = lax.broadcasted_iota(jnp.int32, (1, hw), 1) >= HEAD_DIM
        w_ref[0:rows, :] = jnp.where(lane_hi, 0.0, q).astype(BF16)
        w_ref[rows:, :] = jnp.where(lane_hi, q, 0.0).astype(BF16)
        m_ref[...] = jnp.full_like(m_ref, NEG_INF)
        l_ref[...] = jnp.zeros_like(l_ref)
        acc_ref[...] = jnp.zeros_like(acc_ref)

    def absorb(scores, offsets, values):
        m_prev = m_ref[...]
        m_new = m_prev
        for s, off in zip(scores, offsets):
            m_new = jnp.maximum(m_new, jnp.max(s, axis=-1, keepdims=True) + off)
        alpha = jnp.exp2(m_prev - m_new)
        l = alpha * l_ref[...]
        acc = alpha * acc_ref[...]
        for s, off, v in zip(scores, offsets, values):
            p = jnp.exp2(s - (m_new - off))
            l = l + jnp.sum(p, axis=-1, keepdims=True)
            acc = acc + jnp.dot(p.astype(BF16), v, preferred_element_type=F32)
        l_ref[...] = l
        acc_ref[...] = acc
        m_ref[...] = m_new

    slope = slope_ref[...]
    w = w_ref[...]
    absorb([_nt(w, flat(k_refs[i]).astype(BF16)) + bp_ref[...] for i in range(group)],
           [slope * ((g * group + i + 1 - n_pages) * PAGE_SIZE).astype(F32) for i in range(group)],
           [flat(v_refs[i]).astype(BF16) for i in range(group)])

    @pl.when(g == pl.num_programs(1) - 1)
    def _():
        pad = jnp.zeros((LANES - rows, hw), F32)
        kn = jnp.concatenate([flat(kn_ref), pad], axis=0).astype(BF16)
        vn = jnp.concatenate([flat(vn_ref), pad], axis=0).astype(BF16)
        absorb([_nt(w, kn) + bn_ref[...]], [0.0], [vn])
        lm = lam_ref[...]
        lam = (jnp.exp(jnp.sum(lm[0:1] * lm[1:2], axis=-1, keepdims=True))
               - jnp.exp(jnp.sum(lm[2:3] * lm[3:4], axis=-1, keepdims=True)) + lam_init)
        o = acc_ref[...] * (1.0 / l_ref[...])
        o = o[0:rows] - lam * o[rows:]
        o = o * _rms_rows(o) * sub_ref[...] * (1.0 - lam_init)
        o_ref[0] = o.reshape(n_tok, n_heads, hw)


def _diff_sample(q, k_new, v_new, cache_k, cache_v, page_table, lam, subln, layer_j, *, lam_init, group):
    n, n_tok, n_heads, hw = q.shape
    n_pages = page_table.shape[1]
    assert n_pages % group == 0
    rows = 2 * n_tok * n_heads
    tab_page, tab_new, slope = _diff_sample_tables(n_tok, n_heads)
    fix = lambda i, g, pt: (0, 0)
    tok = pl.BlockSpec((1, n_tok, n_heads, hw), lambda i, g, pt: (i, 0, 0, 0))

    def page_spec(slot):
        return pl.BlockSpec((1, 1, PAGE_SIZE, n_heads, hw),
                            lambda i, g, pt: (layer_j, pt[i, g * group + slot], 0, 0, 0))

    pages = [page_spec(s) for s in range(group)]
    return pl.pallas_call(
        functools.partial(_diff_sample_kernel, lam_init=lam_init, group=group, n_pages=n_pages),
        out_shape=jax.ShapeDtypeStruct((n, n_tok, n_heads, hw), F32),
        grid_spec=pltpu.PrefetchScalarGridSpec(
            num_scalar_prefetch=1,
            grid=(n, n_pages // group),
            in_specs=[pl.BlockSpec(lam.shape, fix), pl.BlockSpec((1, hw), fix), pl.BlockSpec(tab_page.shape, fix),
                      pl.BlockSpec(tab_new.shape, fix), pl.BlockSpec(slope.shape, fix), tok, tok, tok]
                     + pages + pages,
            out_specs=tok,
            scratch_shapes=[pltpu.VMEM((rows, hw), BF16), pltpu.VMEM((rows, 1), F32),
                            pltpu.VMEM((rows, 1), F32), pltpu.VMEM((rows, hw), F32)],
        ),
        compiler_params=_params("parallel", "arbitrary"),
        name="diff_sample",
    )(page_table, lam, subln.reshape(1, hw), tab_page, tab_new, slope, q, k_new, v_new,
      *([cache_k] * group), *([cache_v] * group))


def _by_head(a, *, shp):
    n_tok = shp[1]
    a = a.reshape(shp).transpose(0, 2, 1, 3)
    return jnp.pad(a, ((0, 0), (0, 0), (0, -n_tok % 8), (0, 0)))


def _row_tile(rows, target):
    tile = min(rows, target)
    while rows % tile:
        tile //= 2
    return tile


def kernel(x_prompt, x_sample, state_conv, cache_b_k, cache_b_v, cache_c_k, cache_c_v, page_table, norm_gain,
           ffn_w_gate, ffn_w_up, ffn_w_down, w_in_even, conv_w, qk_gain_b, w_out_even, w_in_odd, qk_gain_c,
           lambda_c, subln_c, w_out_odd):
    batch, seq, d = x_prompt.shape
    n, n_tok, _ = x_sample.shape
    depth = norm_gain.shape[0]
    w_buf = cache_b_k.shape[2]
    xp = x_prompt.reshape(batch * seq, d)
    xs = x_sample.reshape(n * n_tok, d)
    tp, ts = _row_tile(batch * seq, 512), _row_tile(n * n_tok, 512)
    tq = _row_tile(seq, 512)
    wg, wu, wd = ffn_w_gate.astype(BF16), ffn_w_up.astype(BF16), ffn_w_down.astype(BF16)
    conv_p, conv_s, bk_p, bv_p, bk_s, bv_s, ck_p, cv_p, ck_s, cv_s = ([] for _ in range(10))

    for layer in range(depth):
        g = norm_gain[layer]
        j = layer // 2
        xp = _half_ffn(xp, g[0], wg, wu, wd, (layer, 0), tm=tp)
        xs = _half_ffn(xs, g[0], wg, wu, wd, (layer, 0), tm=ts)
        if layer % 2 == 0:
            w_in, w_out = w_in_even[j].astype(BF16), w_out_even[j].astype(BF16)
            cw = w_in.shape[1] // 6
            heads = (cw // HEAD_DIM, HEAD_DIM)
            bg, u, kf, vf, qs, ks, vs = _inproj_even(xp, g[1], w_in, qk_gain_b[j], tm=tp, dilations=DILATIONS)
            res = [_dilated_prompt(qs[b], ks[b], vs[b], batch=batch, dilation=dil) for b, dil in enumerate(DILATIONS)]
            xp = _even_out(xp, bg, u, None, [r[0] for r in res], [r[1] for r in res], conv_w[j], w_out,
                           tm=tp, seq_rows=seq, sample=False)
            conv_p.append(u.reshape(batch, seq, cw)[:, seq - (CONV_K - 1):])
            last = lambda a: a.reshape(batch, seq, cw)[:, seq - w_buf:].reshape(batch, w_buf, *heads)
            bk_p.append(last(kf))
            bv_p.append(last(vf))

            bg, u, kf, vf, (qb,), _, _ = _inproj_even(xs, g[1], w_in, qk_gain_b[j], tm=ts)
            shp = (n, n_tok) + heads
            by_head = functools.partial(_by_head, shp=shp)
            o = _dilated_sample(by_head(qb.astype(F32)), by_head(kf), by_head(vf),
                                cache_b_k.transpose(0, 1, 3, 4, 2), cache_b_v.transpose(0, 1, 3, 4, 2), j,
                                n_tok=n_tok)
            o = o[:, :, :n_tok].transpose(0, 2, 1, 3)
            st = state_conv[j]
            hist2 = jnp.pad(st, ((0, 0), (0, n_tok - (CONV_K - 1)), (0, 0))).reshape(n * n_tok, cw)
            hist1 = jnp.pad(st[:, 1:], ((0, 0), (0, n_tok - 1), (0, 0))).reshape(n * n_tok, cw)
            xs = _even_out(xs, bg, u, (hist1, hist2), [o.reshape(n * n_tok, cw)], [], conv_w[j], w_out,
                           tm=ts, seq_rows=n_tok, sample=True)
            conv_s.append(u.reshape(n, n_tok, cw)[:, n_tok - (CONV_K - 1):])
            bk_s.append(kf.reshape(shp))
            bv_s.append(vf.reshape(shp))
        else:
            w_in, w_out = w_in_odd[j].astype(BF16), w_out_odd[j].astype(BF16)
            cw = w_in.shape[1] // 3
            heads = (cw // (2 * HEAD_DIM), 2 * HEAD_DIM)
            lam_init = 0.8 - 0.6 * math.exp(-0.3 * layer)
            qb, ka, kb, kf, vf, vt = _inproj_odd(xp, g[1], w_in, qk_gain_c[j], tm=tq, kv_block=tq)
            o = _diff_prompt(qb, ka, kb, vt, lambda_c[j], subln_c[j], batch=batch, lam_init=lam_init, tq=tq)
            xp = _outproj(xp, o, w_out, tm=tp)
            ck_p.append(kf.reshape(batch, seq // PAGE_SIZE, PAGE_SIZE, *heads))
            cv_p.append(vf.reshape(batch, seq // PAGE_SIZE, PAGE_SIZE, *heads))

            qb, kf, vf = _inproj_odd(xs, g[1], w_in, qk_gain_c[j], tm=ts)
            shp = (n, n_tok) + heads
            o = _diff_sample(qb.astype(F32).reshape(shp), kf.reshape(shp), vf.reshape(shp), cache_c_k, cache_c_v,
                             page_table, lambda_c[j], subln_c[j], j, lam_init=lam_init,
                             group=math.gcd(page_table.shape[1], 16))
            xs = _outproj(xs, o.reshape(n * n_tok, cw), w_out, tm=ts)
            ck_s.append(kf.reshape(shp))
            cv_s.append(vf.reshape(shp))
        xp = _half_ffn(xp, g[2], wg, wu, wd, (layer, 1), tm=tp)
        xs = _half_ffn(xs, g[2], wg, wu, wd, (layer, 1), tm=ts)

    return (xp.reshape(batch, seq, d), xs.reshape(n, n_tok, d),
            jnp.stack(conv_p), jnp.stack(conv_s),
            jnp.stack(bk_p), jnp.stack(bv_p), jnp.stack(bk_s), jnp.stack(bv_s),
            jnp.stack(ck_p), jnp.stack(cv_p), jnp.stack(ck_s), jnp.stack(cv_s))
```

```python
import functools
import math

import jax
import jax.numpy as jnp
import numpy as np
from jax import lax
from jax.experimental import pallas as pl
from jax.experimental.pallas import tpu as pltpu

F32 = jnp.float32
BF16 = jnp.bfloat16

HEAD_DIM = 64
CONV_K = 3
SUB_WINDOW = 128
DILATIONS = (1, 4, 16)
PAGE_SIZE = 128
ATTN_SCALE = HEAD_DIM ** -0.5
LOG2E = math.log2(math.e)
RMS_EPS = 1e-6
NEG_INF = -1e30

V7X_VMEM_LIMIT_BYTES = 56 * 1024 * 1024
MXU_COLS = 256
LANES = 128


def _params(*sem):
    return pltpu.CompilerParams(dimension_semantics=sem, vmem_limit_bytes=V7X_VMEM_LIMIT_BYTES)


def _rms_rows(x):
    return lax.rsqrt(jnp.mean(x * x, axis=-1, keepdims=True) + RMS_EPS)


def _nt(a, b):
    return lax.dot_general(a, b, (((1,), (1,)), ((), ())), preferred_element_type=F32)


def _group_ones(width, group):
    i = jnp.arange(width) // group
    return (i[:, None] == i[None, :]).astype(BF16)


def _group_rmsnorm(x, bd_ref, gain):
    x2 = x * x
    hi = x2.astype(BF16)
    lo = (x2 - hi.astype(F32)).astype(BF16)
    bd = bd_ref[...]
    cols = []
    for c in range(x.shape[1] // MXU_COLS):
        sl = slice(c * MXU_COLS, (c + 1) * MXU_COLS)
        cols.append(jnp.dot(hi[:, sl], bd, preferred_element_type=F32)
                    + jnp.dot(lo[:, sl], bd, preferred_element_type=F32))
    ss = jnp.concatenate(cols, axis=1)
    return x * lax.rsqrt(ss * (1.0 / HEAD_DIM) + RMS_EPS) * gain


def _ffn_kernel(x_ref, g_ref, wg_ref, wu_ref, wd_ref, o_ref, *, tf):
    x = x_ref[...]
    h = (x * _rms_rows(x) * g_ref[...]).astype(BF16)
    acc = jnp.zeros(x.shape, F32)
    for c in range(wg_ref.shape[1] // tf):
        cols = slice(c * tf, (c + 1) * tf)
        a = jnp.dot(h, wg_ref[:, cols], preferred_element_type=F32)
        b = jnp.dot(h, wu_ref[:, cols], preferred_element_type=F32)
        z = (a * (1.0 / (1.0 + jnp.exp(-a))) * b).astype(BF16)
        acc = acc + jnp.dot(z, wd_ref[cols, :], preferred_element_type=F32)
    o_ref[...] = x + 0.5 * acc


def _half_ffn(x, g, wg, wu, wd, which, *, tm, tf=MXU_COLS):
    t, d = x.shape
    f = wg.shape[-1]
    return pl.pallas_call(
        functools.partial(_ffn_kernel, tf=tf),
        out_shape=jax.ShapeDtypeStruct((t, d), F32),
        grid=(t // tm,),
        in_specs=[
            pl.BlockSpec((tm, d), lambda i: (i, 0)),
            pl.BlockSpec((1, d), lambda i: (0, 0)),
            pl.BlockSpec((None, None, d, f), lambda i: (*which, 0, 0)),
            pl.BlockSpec((None, None, d, f), lambda i: (*which, 0, 0)),
            pl.BlockSpec((None, None, f, d), lambda i: (*which, 0, 0)),
        ],
        out_specs=pl.BlockSpec((tm, d), lambda i: (i, 0)),
        compiler_params=_params("parallel"),
        name="half_ffn",
    )(x, g.reshape(1, d), wg, wu, wd)


def _inproj_even_kernel(x_ref, g_ref, w_ref, bd_ref, gq_ref, gk_ref, bg_ref, u_ref, kf_ref, vf_ref, *rest, dilations):
    n_d = len(dilations)
    stages = rest[3 * n_d:]
    x = x_ref[...]
    h = (x * _rms_rows(x) * g_ref[...]).astype(BF16)
    tm, cw = u_ref.shape

    def col(c):
        return jnp.dot(h, w_ref[:, c * cw:(c + 1) * cw], preferred_element_type=F32)

    bg_ref[...] = col(0)
    u_ref[...] = col(1) * col(2)
    q = _group_rmsnorm(col(3), bd_ref, gq_ref[...])
    k = _group_rmsnorm(col(4), bd_ref, gk_ref[...])
    v = col(5)
    kf_ref[...] = k
    vf_ref[...] = v
    for n, val in enumerate((q * ATTN_SCALE, k, v)):
        stage = stages[n]
        for c in range(cw // LANES):
            stage[c] = val[:, c * LANES:(c + 1) * LANES]
        for d, o_ref in zip(dilations, rest[n * n_d:(n + 1) * n_d]):
            for r in range(d):
                for c in range(cw // LANES):
                    lanes = slice(r * cw + c * LANES, r * cw + (c + 1) * LANES)
                    o_ref[:, lanes] = stage[c, pl.ds(r, tm // d, stride=d), :].astype(BF16)


def _inproj_even(x, g, w, qk_g, *, tm, dilations=(1,)):
    t, d_model = x.shape
    cw = w.shape[1] // 6
    reps = cw // HEAD_DIM
    row = lambda i: (i, 0)
    fix = lambda i: (0, 0)
    n_d = len(dilations)
    outs = [jax.ShapeDtypeStruct((t, cw), F32)] * 4 \
        + [jax.ShapeDtypeStruct((t // d, d * cw), BF16) for d in dilations] * 3
    out_specs = [pl.BlockSpec((tm, cw), row)] * 4 + [pl.BlockSpec((tm // d, d * cw), row) for d in dilations] * 3
    res = pl.pallas_call(
        functools.partial(_inproj_even_kernel, dilations=dilations),
        out_shape=outs,
        grid=(t // tm,),
        in_specs=[
            pl.BlockSpec((tm, d_model), row),
            pl.BlockSpec((1, d_model), fix),
            pl.BlockSpec(w.shape, fix),
            pl.BlockSpec((MXU_COLS, MXU_COLS), fix),
            pl.BlockSpec((1, cw), fix),
            pl.BlockSpec((1, cw), fix),
        ],
        out_specs=out_specs,
        scratch_shapes=[pltpu.VMEM((cw // LANES, tm, LANES), F32)] * 3,
        compiler_params=_params("parallel"),
        name="inproj_even",
    )(x, g.reshape(1, d_model), w, _group_ones(MXU_COLS, HEAD_DIM),
      jnp.tile(qk_g[0], reps).reshape(1, cw), jnp.tile(qk_g[1], reps).reshape(1, cw))
    return list(res[:4]) + [list(res[4 + n * n_d:4 + (n + 1) * n_d]) for n in range(3)]


def _inproj_odd_kernel(x_ref, g_ref, w_ref, bd_ref, gq_ref, gk_ref, *rest, kv_block):
    if kv_block:
        bias_ref, qb_ref, ka_ref, kb_ref, kf_ref, vf_ref, vt_ref = rest
    else:
        qb_ref, kf_ref, vf_ref = rest
    x = x_ref[...]
    h = (x * _rms_rows(x) * g_ref[...]).astype(BF16)
    cw = qb_ref.shape[1]

    def col(c):
        return jnp.dot(h, w_ref[:, c * cw:(c + 1) * cw], preferred_element_type=F32)

    q = _group_rmsnorm(col(0), bd_ref, gq_ref[...])
    k = _group_rmsnorm(col(1), bd_ref, gk_ref[...])
    qb_ref[...] = (q * (ATTN_SCALE * LOG2E)).astype(BF16)
    kf_ref[...] = k
    v = col(2)
    vf_ref[...] = v
    if kv_block:
        first_half = lax.broadcasted_iota(jnp.int32, (1, LANES), 1) < HEAD_DIM
        for hh in range(cw // LANES):
            sl = slice(hh * LANES, (hh + 1) * LANES)
            k_h, tab = k[:, sl], bias_ref[hh]
            ka_ref[:, sl] = jnp.where(first_half, k_h, tab).astype(BF16)
            kb_ref[:, sl] = jnp.where(first_half, pltpu.roll(k_h, HEAD_DIM, 1), tab).astype(BF16)
        vt = v.T.astype(BF16)
        for c in range(vt_ref.shape[0]):
            vt_ref[c] = vt[:, c * kv_block:(c + 1) * kv_block]


BIAS_LANES = 3


def _key_bias_table(n_heads, block):
    slope = np.asarray([_alibi_slope(h, n_heads) * LOG2E for h in range(n_heads)], np.float32)
    rest = slope[:, None] * np.arange(block, dtype=np.float32)[None, :]
    tab = np.zeros((n_heads, block, LANES), np.float32)
    for t in range(BIAS_LANES):
        term = (rest.view(np.uint32) & np.uint32(0xFFFF0000)).view(np.float32)
        tab[:, :, HEAD_DIM + t] = term
        rest = rest - term
    return jnp.asarray(tab)


def _inproj_odd(x, g, w, qk_g, *, tm, kv_block=0):
    t, d = x.shape
    cw = w.shape[1] // 3
    reps = cw // HEAD_DIM
    row = lambda i: (i, 0)
    fix = lambda i: (0, 0)
    ins = [x, g.reshape(1, d), w, _group_ones(MXU_COLS, HEAD_DIM),
           jnp.tile(qk_g[0], reps).reshape(1, cw), jnp.tile(qk_g[1], reps).reshape(1, cw)]
    in_specs = [pl.BlockSpec((tm, d), row), pl.BlockSpec((1, d), fix), pl.BlockSpec(w.shape, fix),
                pl.BlockSpec((MXU_COLS, MXU_COLS), fix), pl.BlockSpec((1, cw), fix), pl.BlockSpec((1, cw), fix)]
    tile = pl.BlockSpec((tm, cw), row)
    tok_bf16, tok_f32 = jax.ShapeDtypeStruct((t, cw), BF16), jax.ShapeDtypeStruct((t, cw), F32)
    if kv_block:
        assert tm == kv_block
        bias = _key_bias_table(cw // LANES, kv_block)
        ins.append(bias)
        in_specs.append(pl.BlockSpec(bias.shape, lambda i: (0, 0, 0)))
        outs = [tok_bf16, tok_bf16, tok_bf16, tok_f32, tok_f32,
                jax.ShapeDtypeStruct((t // kv_block, cw, kv_block), BF16)]
        out_specs = [tile] * 5 + [pl.BlockSpec((tm // kv_block, cw, kv_block), lambda i: (i, 0, 0))]
    else:
        outs = [tok_bf16, tok_f32, tok_f32]
        out_specs = [tile] * 3
    return pl.pallas_call(
        functools.partial(_inproj_odd_kernel, kv_block=kv_block),
        out_shape=outs,
        grid=(t // tm,),
        in_specs=in_specs,
        out_specs=out_specs,
        compiler_params=_params("parallel"),
        name="inproj_odd",
    )(*ins)


def _outproj_kernel(x_ref, a_ref, w_ref, o_ref):
    o_ref[...] = x_ref[...] + jnp.dot(a_ref[...].astype(BF16), w_ref[...], preferred_element_type=F32)


def _outproj(x, a, w, *, tm):
    t, d = x.shape
    row = lambda i: (i, 0)
    return pl.pallas_call(
        _outproj_kernel,
        out_shape=jax.ShapeDtypeStruct((t, d), F32),
        grid=(t // tm,),
        in_specs=[pl.BlockSpec((tm, d), row), pl.BlockSpec((tm, a.shape[1]), row),
                  pl.BlockSpec(w.shape, lambda i: (0, 0))],
        out_specs=pl.BlockSpec((tm, d), row),
        compiler_params=_params("parallel"),
        name="outproj",
    )(x, a, w)


def _alibi_slope(h, n_heads):
    return 2.0 ** (-8.0 * (h + 1) / n_heads)


def _dilated_prompt_kernel(bias_ref, q_ref, kc_ref, kp_ref, vc_ref, vp_ref, o_ref, lse_ref, *, n_heads):
    step = pl.program_id(2)
    tq = kp_ref.shape[1]
    n_sub = q_ref.shape[1] // tq
    lane_hi = lax.broadcasted_iota(jnp.int32, (1, LANES), 1) >= HEAD_DIM
    for sub in range(n_sub):
        rows = slice(sub * tq, (sub + 1) * tq)
        before = slice((sub - 1) * tq, sub * tq)
        variant = jnp.minimum(step, 1) if sub == 0 else 1
        for p in range(n_heads // 2):
            sl = slice(p * LANES, (p + 1) * LANES)
            q = q_ref[0, rows, sl]
            k_prev, v_prev = (kp_ref[0, :, sl], vp_ref[0, :, sl]) if sub == 0 else (kc_ref[0, before, sl],
                                                                                   vc_ref[0, before, sl])
            k = jnp.concatenate([k_prev, kc_ref[0, rows, sl]], axis=0)
            v = jnp.concatenate([v_prev, vc_ref[0, rows, sl]], axis=0)
            o_pair = jnp.zeros((tq, LANES), F32)
            lse_pair = jnp.zeros((tq, LANES), F32)
            for e in range(2):
                keep = lane_hi if e else jnp.logical_not(lane_hi)
                qe = jnp.where(keep, q, jnp.zeros_like(q))
                ve = jnp.where(keep, v, jnp.zeros_like(v))
                s = _nt(qe, k) + bias_ref[variant, 2 * p + e]
                m = jnp.max(s, axis=-1, keepdims=True)
                pr = jnp.exp(s - m)
                l = jnp.sum(pr, axis=-1, keepdims=True)
                o_pair = o_pair + jnp.dot(pr.astype(BF16), ve, preferred_element_type=F32) * (1.0 / l)
                lse_pair = jnp.where(keep, m + jnp.log(l), lse_pair)
            o_ref[0, rows, sl] = o_pair
            lse_ref[0, rows, sl] = lse_pair


def _dilated_bias_table(n_heads, dilation):
    tq = SUB_WINDOW
    qi = jnp.arange(tq)[:, None]
    kj = jnp.arange(2 * tq)[None, :]
    dist = qi + tq - kj
    band = (dist >= 0) & (dist <= SUB_WINDOW)
    slope = jnp.asarray([_alibi_slope(h, n_heads) for h in range(n_heads)], F32)[:, None, None]
    bias = -slope * (dilation * dist).astype(F32)[None]
    return jnp.stack([jnp.where(band & (kj >= tq), bias, NEG_INF), jnp.where(band, bias, NEG_INF)])


def _dilated_prompt(q, k, v, *, batch, dilation):
    d = dilation
    t, width = q.shape[0] * d, q.shape[1] // d
    s_len = t // batch
    tq = SUB_WINDOW
    n_sub = max(c for c in (1, 2, 4) if s_len % (c * d * tq) == 0)
    steps = s_len // (d * tq * n_sub)
    assert steps * d * tq * n_sub == s_len
    n_heads = width // HEAD_DIM
    bias = _dilated_bias_table(n_heads, d)
    view = lambda a: a.reshape(batch, s_len // d, d * width)
    cur = lambda b, r, i: (b, i, r)
    prev = lambda b, r, i: (b, jnp.maximum(i * n_sub - 1, 0), r)
    blk, halo = (1, n_sub * tq, width), (1, tq, width)
    o, lse = pl.pallas_call(
        functools.partial(_dilated_prompt_kernel, n_heads=n_heads),
        out_shape=[jax.ShapeDtypeStruct((batch, s_len // d, d * width), F32)] * 2,
        grid=(batch, d, steps),
        in_specs=[pl.BlockSpec(bias.shape, lambda b, r, i: (0, 0, 0, 0)),
                  pl.BlockSpec(blk, cur), pl.BlockSpec(blk, cur), pl.BlockSpec(halo, prev),
                  pl.BlockSpec(blk, cur), pl.BlockSpec(halo, prev)],
        out_specs=[pl.BlockSpec(blk, cur)] * 2,
        compiler_params=_params("parallel", "parallel", "arbitrary"),
        name=f"dilated_prompt_d{d}",
    )(bias, view(q), view(k), view(k), view(v), view(v))
    return o.reshape(t // d, d * width), lse.reshape(t // d, d * width)


def _even_out_kernel(x_ref, bg_ref, u_ref, *rest, seq_rows, sample):
    if sample:
        s1_ref, s2_ref = rest[:2]
        rest = rest[2:]
    else:
        halo_ref = rest[0]
        rest = rest[1:]
    tm, cw = u_ref.shape
    if sample:
        o1_ref, cw_ref, w_ref, out_ref = rest
    else:
        n_d = len(DILATIONS)
        branch_refs, (cw_ref, w_ref, out_ref), stages = rest[:2 * n_d], rest[2 * n_d:2 * n_d + 3], rest[2 * n_d + 3:]

        def token_rows(n):
            d, g_ref = DILATIONS[n % n_d], branch_refs[n]
            if d == 1:
                return g_ref[...]
            stage = stages[n]
            for r in range(d):
                for c in range(cw // LANES):
                    lanes = slice(r * cw + c * LANES, r * cw + (c + 1) * LANES)
                    stage[c, pl.ds(r, tm // d, stride=d), :] = g_ref[:, lanes]
            return jnp.concatenate([stage[c] for c in range(cw // LANES)], axis=1)
    u = u_ref[...]
    r1 = pltpu.roll(u, 1, 0)
    r2 = pltpu.roll(u, 2, 0)
    if sample:
        t_in_seq = lax.broadcasted_iota(jnp.int32, (tm, cw), 0) % seq_rows
        u1 = jnp.where(t_in_seq >= 1, r1, s1_ref[...])
        u2 = jnp.where(t_in_seq >= 2, r2, s2_ref[...])
    else:
        first = (pl.program_id(0) % (seq_rows // tm)) == 0
        halo = jnp.where(first, 0.0, halo_ref[...])
        row8 = lax.broadcasted_iota(jnp.int32, halo.shape, 0)
        top1 = jnp.where(row8 < 1, pltpu.roll(halo, 1, 0), r1[0:8])
        top2 = jnp.where(row8 < 2, pltpu.roll(halo, 2, 0), r2[0:8])
        u1 = jnp.concatenate([top1, r1[8:]], axis=0)
        u2 = jnp.concatenate([top2, r2[8:]], axis=0)
    cwt = cw_ref[...]
    a = bg_ref[...] * (cwt[0:1] * u2 + cwt[1:2] * u1 + cwt[2:3] * u)
    if sample:
        b = o1_ref[...]
    else:
        o1, o2, o3, l1, l2, l3 = (token_rows(n) for n in range(2 * n_d))
        mx = jnp.maximum(jnp.maximum(l1, l2), l3)
        e1, e2, e3 = jnp.exp(l1 - mx), jnp.exp(l2 - mx), jnp.exp(l3 - mx)
        b = (e1 * o1 + e2 * o2 + e3 * o3) * (1.0 / (e1 + e2 + e3))
    out_ref[...] = (x_ref[...]
                    + jnp.dot(a.astype(BF16), w_ref[0:cw, :], preferred_element_type=F32)
                    + jnp.dot(b.astype(BF16), w_ref[cw:, :], preferred_element_type=F32))


def _even_out(x, bg, u, hist, outs, lses, conv_w, w_out, *, tm, seq_rows, sample):
    t, d = x.shape
    cw = u.shape[1]
    row = lambda i: (i, 0)
    fix = lambda i: (0, 0)
    tile = pl.BlockSpec((tm, cw), row)
    if sample:
        hist_in, hist_specs = list(hist), [tile, tile]
        branch_specs, scratch = [tile], []
    else:
        hist_in = [u]
        hist_specs = [pl.BlockSpec((8, cw), lambda i: (jnp.maximum(i * (tm // 8) - 1, 0), 0))]
        branch_specs = [pl.BlockSpec((tm // dil, dil * cw), row) for dil in DILATIONS] * 2
        scratch = [pltpu.VMEM((cw // LANES, tm, LANES), F32)] * (2 * len(DILATIONS))
    return pl.pallas_call(
        functools.partial(_even_out_kernel, seq_rows=seq_rows, sample=sample),
        out_shape=jax.ShapeDtypeStruct((t, d), F32),
        grid=(t // tm,),
        in_specs=[pl.BlockSpec((tm, d), row), tile, tile] + hist_specs + branch_specs
                 + [pl.BlockSpec(conv_w.shape, fix), pl.BlockSpec(w_out.shape, fix)],
        out_specs=pl.BlockSpec((tm, d), row),
        scratch_shapes=scratch,
        compiler_params=_params("parallel"),
        name="even_out",
    )(x, bg, u, *hist_in, *outs, *lses, conv_w, w_out)


def _diff_prompt_kernel(lam_ref, sub_ref, slope_ref, q_ref, ka_ref, kb_ref, vt_ref, o_ref,
                        qs_ref, sa_ref, sb_ref, ma_ref, mb_ref, m_ref, l_ref, acc_ref, *, lam_init):
    i = pl.program_id(2)
    tq = q_ref.shape[1]
    n_grp = 2 * tq // MXU_COLS
    q = q_ref[0].astype(F32)
    lane = lax.broadcasted_iota(jnp.int32, (1, LANES), 1)
    ones = jnp.where((lane >= HEAD_DIM) & (lane < HEAD_DIM + BIAS_LANES), 1.0, 0.0)
    qs_ref[0:tq, :] = jnp.where(lane < HEAD_DIM, q, ones).astype(BF16)
    qs_ref[tq:, :] = jnp.where(lane < HEAD_DIM, pltpu.roll(q, HEAD_DIM, 1), ones).astype(BF16)
    m_ref[...] = jnp.full_like(m_ref, NEG_INF)
    l_ref[...] = jnp.zeros_like(l_ref)
    acc_ref[...] = jnp.zeros_like(acc_ref)
    slope = slope_ref[0][:, 0:1]
    key_row = lax.broadcasted_iota(jnp.int32, (tq, LANES), 0)
    q_col = lax.broadcasted_iota(jnp.int32, (tq, LANES), 1)

    buf_a, buf_b = (sa_ref, ma_ref), (sb_ref, mb_ref)

    def score(j, buf):
        s_ref, smax_ref = buf
        rows = pl.ds(pl.multiple_of(j * tq, tq), tq)
        for c in range(n_grp):
            cols = slice(c * MXU_COLS, (c + 1) * MXU_COLS)
            k_ref = ka_ref if c < n_grp // 2 else kb_ref
            s = _nt(k_ref[0, rows, :], qs_ref[cols, :])
            s_ref[:, cols] = s
            smax_ref[:, cols] = jnp.max(s, axis=0, keepdims=True)

    def absorb(j, buf, diagonal):
        s_ref, smax_ref = buf
        vt = vt_ref[j]
        off = slope * ((j - i) * tq).astype(F32)
        m_all, l_all = m_ref[...], l_ref[...]
        m_out, l_out = [], []
        for c in range(n_grp):
            cols = slice(c * MXU_COLS, (c + 1) * MXU_COLS)
            s = s_ref[:, cols]
            if diagonal:
                s = jnp.concatenate(
                    [jnp.where(key_row <= q_col + (c * MXU_COLS + cc * LANES) % tq,
                               s[:, cc * LANES:(cc + 1) * LANES], NEG_INF) for cc in range(MXU_COLS // LANES)], axis=1)
                smax = jnp.max(s, axis=0, keepdims=True)
            else:
                smax = smax_ref[:, cols]
            m_prev = m_all[:, cols]
            m_new = jnp.maximum(m_prev, smax + off)
            alpha = jnp.exp2(m_prev - m_new)
            p = jnp.exp2(s - (m_new - off))
            l_out.append(alpha * l_all[:, cols] + jnp.sum(p, axis=0, keepdims=True))
            acc_ref[:, cols] = alpha * acc_ref[:, cols] + jnp.dot(vt, p.astype(BF16), preferred_element_type=F32)
            m_out.append(m_new)
        m_ref[...] = jnp.concatenate(m_out, axis=1)
        l_ref[...] = jnp.concatenate(l_out, axis=1)

    score(0, buf_a)

    def pair(jj, carry):
        j = 2 * jj
        score(j + 1, buf_b)
        absorb(j, buf_a, False)
        score(j + 2, buf_a)
        absorb(j + 1, buf_b, False)
        return carry

    lax.fori_loop(0, i // 2, pair, 0)

    @pl.when(i % 2 == 1)
    def _():
        score(i, buf_b)
        absorb(i - 1, buf_a, False)
        absorb(i, buf_b, True)

    @pl.when(i % 2 == 0)
    def _():
        absorb(i, buf_a, True)

    lm = lam_ref[...]
    lam = (jnp.exp(jnp.sum(lm[0:1] * lm[1:2], axis=-1, keepdims=True))
           - jnp.exp(jnp.sum(lm[2:3] * lm[3:4], axis=-1, keepdims=True)) + lam_init)
    acc = acc_ref[...]
    inv = 1.0 / l_ref[...]
    o_t = acc[:, :tq] * inv[:, :tq] - lam * (acc[:, tq:] * inv[:, tq:])
    o = o_t.T
    o = o * _rms_rows(o) * sub_ref[...] * (1.0 - lam_init)
    o_ref[0] = o.astype(BF16)


def _diff_prompt(q, ka, kb, vt, lam, subln, *, batch, lam_init, tq):
    t, width = q.shape
    s_len = t // batch
    hw = 2 * HEAD_DIM
    n_heads = width // hw
    nq = s_len // tq
    assert vt.shape == (t // tq, width, tq) and nq * tq == s_len
    slopes = jnp.asarray([_alibi_slope(h, n_heads) * LOG2E for h in range(n_heads)], F32)
    slope_tab = jnp.broadcast_to(slopes[:, None, None], (n_heads, 1, LANES))
    out = pl.pallas_call(
        functools.partial(_diff_prompt_kernel, lam_init=lam_init),
        out_shape=jax.ShapeDtypeStruct((batch, s_len, width), BF16),
        grid=(batch, n_heads, nq),
        in_specs=[
            pl.BlockSpec(lam.shape, lambda b, h, i: (0, 0)),
            pl.BlockSpec((1, hw), lambda b, h, i: (0, 0)),
            pl.BlockSpec((1, 1, LANES), lambda b, h, i: (h, 0, 0)),
            pl.BlockSpec((1, tq, hw), lambda b, h, i: (b, i, h)),
            pl.BlockSpec((1, s_len, hw), lambda b, h, i: (b, 0, h)),
            pl.BlockSpec((1, s_len, hw), lambda b, h, i: (b, 0, h)),
            pl.BlockSpec((nq, hw, tq), lambda b, h, i: (b, h, 0)),
        ],
        out_specs=pl.BlockSpec((1, tq, hw), lambda b, h, i: (b, i, h)),
        scratch_shapes=[pltpu.VMEM((2 * tq, hw), BF16), pltpu.VMEM((tq, 2 * tq), F32), pltpu.VMEM((tq, 2 * tq), F32)]
                       + [pltpu.VMEM((1, 2 * tq), F32)] * 4 + [pltpu.VMEM((hw, 2 * tq), F32)],
        compiler_params=_params("parallel", "parallel", "arbitrary"),
        name="diff_prompt",
    )(lam, subln.reshape(1, hw), slope_tab, *(a.reshape(batch, s_len, width) for a in (q, ka, kb)), vt)
    return out.reshape(t, width)


SAMPLE_ROWS = 16


def _dilated_sample_tables(n_tok, w_buf):
    def tables(dist, valid):
        count = sum(((dist % d == 0) & (dist <= d * SUB_WINDOW)).astype(F32) for d in DILATIONS)
        held = valid & (count > 0)
        return (jnp.where(held, dist.astype(F32), -NEG_INF),
                jnp.where(held, jnp.log(jnp.maximum(count, 1.0)), 0.0))

    t = jnp.arange(SAMPLE_ROWS)[:, None]
    j = jnp.arange(LANES)[None, :]
    cache = tables(w_buf + t - jnp.arange(w_buf)[None, :], t < n_tok)
    new = tables(t - j, (t < n_tok) & (j <= t))
    return cache + new


def _dilated_sample_kernel(dc_ref, lc_ref, dn_ref, ln_ref, q_ref, kn_ref, vn_ref, kt_ref, vt_ref, o_ref):
    n_seq, n_heads, n_tok, hd = q_ref.shape
    pad_q = jnp.zeros((SAMPLE_ROWS - n_tok, hd), F32)
    pad_n = jnp.zeros((LANES - n_tok, hd), F32)
    rmax = lambda a: jnp.max(a, axis=-1, keepdims=True)
    rsum = lambda a: jnp.sum(a, axis=-1, keepdims=True)
    for b in range(n_seq):
        for h in range(n_heads):
            slope = _alibi_slope(h, n_heads)
            q = jnp.concatenate([q_ref[b, h], pad_q], axis=0).astype(BF16)
            kn = jnp.concatenate([kn_ref[b, h], pad_n], axis=0).astype(BF16)
            vn = jnp.concatenate([vn_ref[b, h], pad_n], axis=0).astype(BF16)
            s = jnp.dot(q, kt_ref[0, b, h].astype(BF16), preferred_element_type=F32)
            a = s - slope * dc_ref[...] + lc_ref[...]
            an = _nt(q, kn) - slope * dn_ref[...] + ln_ref[...]
            m = jnp.maximum(rmax(a), rmax(an))
            e = jnp.exp(a - m)
            en = jnp.exp(an - m)
            o = _nt(e.astype(BF16), vt_ref[0, b, h].astype(BF16)) + jnp.dot(en.astype(BF16), vn,
                                                                            preferred_element_type=F32)
            o_ref[b, h] = (o * (1.0 / (rsum(e) + rsum(en))))[0:n_tok]


def _dilated_sample(q, k_new, v_new, cache_kt, cache_vt, layer_j, *, n_tok):
    n, n_heads, tok_pad, hd = q.shape
    w_buf = cache_kt.shape[-1]
    assert n_tok <= min(DILATIONS[1:]) and w_buf == max(DILATIONS) * SUB_WINDOW
    tabs = _dilated_sample_tables(n_tok, w_buf)
    n_seq = 2 if n % 2 == 0 else 1
    fix = lambda i: (0, 0)
    tok = pl.BlockSpec((n_seq, n_heads, tok_pad, hd), lambda i: (i, 0, 0, 0))
    blk = pl.BlockSpec((1, n_seq, n_heads, hd, w_buf), lambda i: (layer_j, i, 0, 0, 0))
    return pl.pallas_call(
        _dilated_sample_kernel,
        out_shape=jax.ShapeDtypeStruct((n, n_heads, tok_pad, hd), F32),
        grid=(n // n_seq,),
        in_specs=[pl.BlockSpec(tb.shape, fix) for tb in tabs] + [tok, tok, tok, blk, blk],
        out_specs=tok,
        compiler_params=_params("parallel"),
        name="dilated_sample",
    )(*tabs, q, k_new, v_new, cache_kt, cache_vt)


def _diff_sample_tables(n_tok, n_heads):
    r = jnp.arange(2 * n_tok * n_heads)[:, None]
    t, h = (r // n_heads) % n_tok, r % n_heads
    slope = (jnp.exp2(-8.0 * (h + 1) / n_heads) * LOG2E).astype(F32)
    c = jnp.arange(PAGE_SIZE * n_heads)[None, :]
    tab_page = jnp.where(c % n_heads == h, slope * (c // n_heads - PAGE_SIZE).astype(F32), NEG_INF)
    cn = jnp.arange(LANES)[None, :]
    j = cn // n_heads
    tab_new = jnp.where((cn < n_tok * n_heads) & (cn % n_heads == h) & (j <= t), slope * j.astype(F32), NEG_INF)
    return tab_page, tab_new, slope


def _diff_sample_kernel(pt_ref, lam_ref, sub_ref, bp_ref, bn_ref, slope_ref, q_ref, kn_ref, vn_ref, *rest,
                        lam_init, group, n_pages):
    k_refs, v_refs = rest[:group], rest[group:2 * group]
    o_ref, w_ref, m_ref, l_ref, acc_ref = rest[2 * group:]
    g = pl.program_id(1)
    n_tok, n_heads, hw = q_ref.shape[1:]
    rows = n_tok * n_heads
    flat = lambda ref: ref[...].reshape(-1, hw)

    @pl.when(g == 0)
    def _():
        q = flat(q_ref)
        lane_hi = lax.broadcasted_iota(jnp.int32, (1, hw), 1) >= HEAD_DIM
        w_ref[0:rows, :] = jnp.where(lane_hi, 0.0, q).astype(BF16)
        w_ref[rows:, :] = jnp.where(lane_hi, q, 0.0).astype(BF16)
        m_ref[...] = jnp.full_like(m_ref, NEG_INF)
        l_ref[...] = jnp.zeros_like(l_ref)
        acc_ref[...] = jnp.zeros_like(acc_ref)

    def absorb(scores, offsets, values):
        m_prev = m_ref[...]
        m_new = m_prev
        for s, off in zip(scores, offsets):
            m_new = jnp.maximum(m_new, jnp.max(s, axis=-1, keepdims=True) + off)
        alpha = jnp.exp2(m_prev - m_new)
        l = alpha * l_ref[...]
        acc = alpha * acc_ref[...]
        for s, off, v in zip(scores, offsets, values):
            p = jnp.exp2(s - (m_new - off))
            l = l + jnp.sum(p, axis=-1, keepdims=True)
            acc = acc + jnp.dot(p.astype(BF16), v, preferred_element_type=F32)
        l_ref[...] = l
        acc_ref[...] = acc
        m_ref[...] = m_new

    slope = slope_ref[...]
    w = w_ref[...]
    absorb([_nt(w, flat(k_refs[i]).astype(BF16)) + bp_ref[...] for i in range(group)],
           [slope * ((g * group + i + 1 - n_pages) * PAGE_SIZE).astype(F32) for i in range(group)],
           [flat(v_refs[i]).astype(BF16) for i in range(group)])

    @pl.when(g == pl.num_programs(1) - 1)
    def _():
        pad = jnp.zeros((LANES - rows, hw), F32)
        kn = jnp.concatenate([flat(kn_ref), pad], axis=0).astype(BF16)
        vn = jnp.concatenate([flat(vn_ref), pad], axis=0).astype(BF16)
        absorb([_nt(w, kn) + bn_ref[...]], [0.0], [vn])
        lm = lam_ref[...]
        lam = (jnp.exp(jnp.sum(lm[0:1] * lm[1:2], axis=-1, keepdims=True))
               - jnp.exp(jnp.sum(lm[2:3] * lm[3:4], axis=-1, keepdims=True)) + lam_init)
        o = acc_ref[...] * (1.0 / l_ref[...])
        o = o[0:rows] - lam * o[rows:]
        o = o * _rms_rows(o) * sub_ref[...] * (1.0 - lam_init)
        o_ref[0] = o.reshape(n_tok, n_heads, hw)


def _diff_sample(q, k_new, v_new, cache_k, cache_v, page_table, lam, subln, layer_j, *, lam_init, group):
    n, n_tok, n_heads, hw = q.shape
    n_pages = page_table.shape[1]
    assert n_pages % group == 0
    rows = 2 * n_tok * n_heads
    tab_page, tab_new, slope = _diff_sample_tables(n_tok, n_heads)
    fix = lambda i, g, pt: (0, 0)
    tok = pl.BlockSpec((1, n_tok, n_heads, hw), lambda i, g, pt: (i, 0, 0, 0))

    def page_spec(slot):
        return pl.BlockSpec((1, 1, PAGE_SIZE, n_heads, hw),
                            lambda i, g, pt: (layer_j, pt[i, g * group + slot], 0, 0, 0))

    pages = [page_spec(s) for s in range(group)]
    return pl.pallas_call(
        functools.partial(_diff_sample_kernel, lam_init=lam_init, group=group, n_pages=n_pages),
        out_shape=jax.ShapeDtypeStruct((n, n_tok, n_heads, hw), F32),
        grid_spec=pltpu.PrefetchScalarGridSpec(
            num_scalar_prefetch=1,
            grid=(n, n_pages // group),
            in_specs=[pl.BlockSpec(lam.shape, fix), pl.BlockSpec((1, hw), fix), pl.BlockSpec(tab_page.shape, fix),
                      pl.BlockSpec(tab_new.shape, fix), pl.BlockSpec(slope.shape, fix), tok, tok, tok]
                     + pages + pages,
            out_specs=tok,
            scratch_shapes=[pltpu.VMEM((rows, hw), BF16), pltpu.VMEM((rows, 1), F32),
                            pltpu.VMEM((rows, 1), F32), pltpu.VMEM((rows, hw), F32)],
        ),
        compiler_params=_params("parallel", "arbitrary"),
        name="diff_sample",
    )(page_table, lam, subln.reshape(1, hw), tab_page, tab_new, slope, q, k_new, v_new,
      *([cache_k] * group), *([cache_v] * group))


def _by_head(a, *, shp):
    n_tok = shp[1]
    a = a.reshape(shp).transpose(0, 2, 1, 3)
    return jnp.pad(a, ((0, 0), (0, 0), (0, -n_tok % 8), (0, 0)))


def _row_tile(rows, target):
    tile = min(rows, target)
    while rows % tile:
        tile //= 2
    return tile


def kernel(x_prompt, x_sample, state_conv, cache_b_k, cache_b_v, cache_c_k, cache_c_v, page_table, norm_gain,
           ffn_w_gate, ffn_w_up, ffn_w_down, w_in_even, conv_w, qk_gain_b, w_out_even, w_in_odd, qk_gain_c,
           lambda_c, subln_c, w_out_odd):
    batch, seq, d = x_prompt.shape
    n, n_tok, _ = x_sample.shape
    depth = norm_gain.shape[0]
    w_buf = cache_b_k.shape[2]
    xp = x_prompt.reshape(batch * seq, d)
    xs = x_sample.reshape(n * n_tok, d)
    tp, ts = _row_tile(batch * seq, 512), _row_tile(n * n_tok, 512)
    tq = _row_tile(seq, 512)
    wg, wu, wd = ffn_w_gate.astype(BF16), ffn_w_up.astype(BF16), ffn_w_down.astype(BF16)
    conv_p, conv_s, bk_p, bv_p, bk_s, bv_s, ck_p, cv_p, ck_s, cv_s = ([] for _ in range(10))

    for layer in range(depth):
        g = norm_gain[layer]
        j = layer // 2
        xp = _half_ffn(xp, g[0], wg, wu, wd, (layer, 0), tm=tp)
        xs = _half_ffn(xs, g[0], wg, wu, wd, (layer, 0), tm=ts)
        if layer % 2 == 0:
            w_in, w_out = w_in_even[j].astype(BF16), w_out_even[j].astype(BF16)
            cw = w_in.shape[1] // 6
            heads = (cw // HEAD_DIM, HEAD_DIM)
            bg, u, kf, vf, qs, ks, vs = _inproj_even(xp, g[1], w_in, qk_gain_b[j], tm=tp, dilations=DILATIONS)
            res = [_dilated_prompt(qs[b], ks[b], vs[b], batch=batch, dilation=dil) for b, dil in enumerate(DILATIONS)]
            xp = _even_out(xp, bg, u, None, [r[0] for r in res], [r[1] for r in res], conv_w[j], w_out,
                           tm=tp, seq_rows=seq, sample=False)
            conv_p.append(u.reshape(batch, seq, cw)[:, seq - (CONV_K - 1):])
            last = lambda a: a.reshape(batch, seq, cw)[:, seq - w_buf:].reshape(batch, w_buf, *heads)
            bk_p.append(last(kf))
            bv_p.append(last(vf))

            bg, u, kf, vf, (qb,), _, _ = _inproj_even(xs, g[1], w_in, qk_gain_b[j], tm=ts)
            shp = (n, n_tok) + heads
            by_head = functools.partial(_by_head, shp=shp)
            o = _dilated_sample(by_head(qb.astype(F32)), by_head(kf), by_head(vf),
                                cache_b_k.transpose(0, 1, 3, 4, 2), cache_b_v.transpose(0, 1, 3, 4, 2), j,
                                n_tok=n_tok)
            o = o[:, :, :n_tok].transpose(0, 2, 1, 3)
            st = state_conv[j]
            hist2 = jnp.pad(st, ((0, 0), (0, n_tok - (CONV_K - 1)), (0, 0))).reshape(n * n_tok, cw)
            hist1 = jnp.pad(st[:, 1:], ((0, 0), (0, n_tok - 1), (0, 0))).reshape(n * n_tok, cw)
            xs = _even_out(xs, bg, u, (hist1, hist2), [o.reshape(n * n_tok, cw)], [], conv_w[j], w_out,
                           tm=ts, seq_rows=n_tok, sample=True)
            conv_s.append(u.reshape(n, n_tok, cw)[:, n_tok - (CONV_K - 1):])
            bk_s.append(kf.reshape(shp))
            bv_s.append(vf.reshape(shp))
        else:
            w_in, w_out = w_in_odd[j].astype(BF16), w_out_odd[j].astype(BF16)
            cw = w_in.shape[1] // 3
            heads = (cw // (2 * HEAD_DIM), 2 * HEAD_DIM)
            lam_init = 0.8 - 0.6 * math.exp(-0.3 * layer)
            qb, ka, kb, kf, vf, vt = _inproj_odd(xp, g[1], w_in, qk_gain_c[j], tm=tq, kv_block=tq)
            o = _diff_prompt(qb, ka, kb, vt, lambda_c[j], subln_c[j], batch=batch, lam_init=lam_init, tq=tq)
            xp = _outproj(xp, o, w_out, tm=tp)
            ck_p.append(kf.reshape(batch, seq // PAGE_SIZE, PAGE_SIZE, *heads))
            cv_p.append(vf.reshape(batch, seq // PAGE_SIZE, PAGE_SIZE, *heads))

            qb, kf, vf = _inproj_odd(xs, g[1], w_in, qk_gain_c[j], tm=ts)
            shp = (n, n_tok) + heads
            o = _diff_sample(qb.astype(F32).reshape(shp), kf.reshape(shp), vf.reshape(shp), cache_c_k, cache_c_v,
                             page_table, lambda_c[j], subln_c[j], j, lam_init=lam_init,
                             group=math.gcd(page_table.shape[1], 16))
            xs = _outproj(xs, o.reshape(n * n_tok, cw), w_out, tm=ts)
            ck_s.append(kf.reshape(shp))
            cv_s.append(vf.reshape(shp))
        xp = _half_ffn(xp, g[2], wg, wu, wd, (layer, 1), tm=tp)
        xs = _half_ffn(xs, g[2], wg, wu, wd, (layer, 1), tm=ts)

    return (xp.reshape(batch, seq, d), xs.reshape(n, n_tok, d),
            jnp.stack(conv_p), jnp.stack(conv_s),
            jnp.stack(bk_p), jnp.stack(bv_p), jnp.stack(bk_s), jnp.stack(bv_s),
            jnp.stack(ck_p), jnp.stack(cv_p), jnp.stack(ck_s), jnp.stack(cv_s))
```

```python
import functools
import math

import jax
import jax.numpy as jnp
import numpy as np
from jax import lax
from jax.experimental import pallas as pl
from jax.experimental.pallas import tpu as pltpu

F32 = jnp.float32
BF16 = jnp.bfloat16

HEAD_DIM = 64
CONV_K = 3
SUB_WINDOW = 128
DILATIONS = (1, 4, 16)
PAGE_SIZE = 128
ATTN_SCALE = HEAD_DIM ** -0.5
LOG2E = math.log2(math.e)
RMS_EPS = 1e-6
NEG_INF = -1e30

V7X_VMEM_LIMIT_BYTES = 56 * 1024 * 1024
MXU_COLS = 256
LANES = 128


def _params(*sem):
    return pltpu.CompilerParams(dimension_semantics=sem, vmem_limit_bytes=V7X_VMEM_LIMIT_BYTES)


def _rms_rows(x):
    return lax.rsqrt(jnp.mean(x * x, axis=-1, keepdims=True) + RMS_EPS)


def _nt(a, b):
    return lax.dot_general(a, b, (((1,), (1,)), ((), ())), preferred_element_type=F32)


def _group_ones(width, group):
    i = jnp.arange(width) // group
    return (i[:, None] == i[None, :]).astype(BF16)


def _group_rmsnorm(x, bd_ref, gain):
    x2 = x * x
    hi = x2.astype(BF16)
    lo = (x2 - hi.astype(F32)).astype(BF16)
    bd = bd_ref[...]
    cols = []
    for c in range(x.shape[1] // MXU_COLS):
        sl = slice(c * MXU_COLS, (c + 1) * MXU_COLS)
        cols.append(jnp.dot(hi[:, sl], bd, preferred_element_type=F32)
                    + jnp.dot(lo[:, sl], bd, preferred_element_type=F32))
    ss = jnp.concatenate(cols, axis=1)
    return x * lax.rsqrt(ss * (1.0 / HEAD_DIM) + RMS_EPS) * gain


def _ffn_kernel(x_ref, g_ref, wg_ref, wu_ref, wd_ref, *rest, tf):
    o_ref = rest[-1]
    x = x_ref[...]
    if len(rest) == 3:
        a_ref, wo_ref = rest[:2]
        x = x + jnp.dot(a_ref[...].astype(BF16), wo_ref[...], preferred_element_type=F32)
    h = (x * _rms_rows(x) * g_ref[...]).astype(BF16)
    acc = jnp.zeros(x.shape, F32)
    for c in range(wg_ref.shape[1] // tf):
        cols = slice(c * tf, (c + 1) * tf)
        a = jnp.dot(h, wg_ref[:, cols], preferred_element_type=F32)
        b = jnp.dot(h, wu_ref[:, cols], preferred_element_type=F32)
        z = (a * (1.0 / (1.0 + jnp.exp(-a))) * b).astype(BF16)
        acc = acc + jnp.dot(z, wd_ref[cols, :], preferred_element_type=F32)
    o_ref[...] = x + 0.5 * acc


def _half_ffn(x, g, wg, wu, wd, which, *, tm, tf=MXU_COLS, proj=()):
    t, d = x.shape
    f = wg.shape[-1]
    proj_specs = [pl.BlockSpec((tm, proj[0].shape[1]), lambda i: (i, 0)),
                  pl.BlockSpec(proj[1].shape, lambda i: (0, 0))] if proj else []
    return pl.pallas_call(
        functools.partial(_ffn_kernel, tf=tf),
        out_shape=jax.ShapeDtypeStruct((t, d), F32),
        grid=(t // tm,),
        in_specs=[
            pl.BlockSpec((tm, d), lambda i: (i, 0)),
            pl.BlockSpec((1, d), lambda i: (0, 0)),
            pl.BlockSpec((None, None, d, f), lambda i: (*which, 0, 0)),
            pl.BlockSpec((None, None, d, f), lambda i: (*which, 0, 0)),
            pl.BlockSpec((None, None, f, d), lambda i: (*which, 0, 0)),
        ] + proj_specs,
        out_specs=pl.BlockSpec((tm, d), lambda i: (i, 0)),
        compiler_params=_params("parallel"),
        name="half_ffn",
    )(x, g.reshape(1, d), wg, wu, wd, *proj)


def _inproj_even_kernel(x_ref, g_ref, w_ref, bd_ref, gq_ref, gk_ref, bg_ref, u_ref, kf_ref, vf_ref, *rest, dilations):
    n_d = len(dilations)
    stages = rest[3 * n_d:]
    x = x_ref[...]
    h = (x * _rms_rows(x) * g_ref[...]).astype(BF16)
    tm, cw = u_ref.shape

    def col(c):
        return jnp.dot(h, w_ref[:, c * cw:(c + 1) * cw], preferred_element_type=F32)

    bg_ref[...] = col(0)
    u_ref[...] = col(1) * col(2)
    q = _group_rmsnorm(col(3), bd_ref, gq_ref[...])
    k = _group_rmsnorm(col(4), bd_ref, gk_ref[...])
    v = col(5)
    kf_ref[...] = k
    vf_ref[...] = v
    for n, val in enumerate((q * ATTN_SCALE, k, v)):
        stage = stages[n]
        for c in range(cw // LANES):
            stage[c] = val[:, c * LANES:(c + 1) * LANES]
        for d, o_ref in zip(dilations, rest[n * n_d:(n + 1) * n_d]):
            for r in range(d):
                for c in range(cw // LANES):
                    lanes = slice(r * cw + c * LANES, r * cw + (c + 1) * LANES)
                    o_ref[:, lanes] = stage[c, pl.ds(r, tm // d, stride=d), :].astype(BF16)


def _inproj_even(x, g, w, qk_g, *, tm, dilations=(1,)):
    t, d_model = x.shape
    cw = w.shape[1] // 6
    reps = cw // HEAD_DIM
    row = lambda i: (i, 0)
    fix = lambda i: (0, 0)
    n_d = len(dilations)
    outs = [jax.ShapeDtypeStruct((t, cw), F32)] * 4 \
        + [jax.ShapeDtypeStruct((t // d, d * cw), BF16) for d in dilations] * 3
    out_specs = [pl.BlockSpec((tm, cw), row)] * 4 + [pl.BlockSpec((tm // d, d * cw), row) for d in dilations] * 3
    res = pl.pallas_call(
        functools.partial(_inproj_even_kernel, dilations=dilations),
        out_shape=outs,
        grid=(t // tm,),
        in_specs=[
            pl.BlockSpec((tm, d_model), row),
            pl.BlockSpec((1, d_model), fix),
            pl.BlockSpec(w.shape, fix),
            pl.BlockSpec((MXU_COLS, MXU_COLS), fix),
            pl.BlockSpec((1, cw), fix),
            pl.BlockSpec((1, cw), fix),
        ],
        out_specs=out_specs,
        scratch_shapes=[pltpu.VMEM((cw // LANES, tm, LANES), F32)] * 3,
        compiler_params=_params("parallel"),
        name="inproj_even",
    )(x, g.reshape(1, d_model), w, _group_ones(MXU_COLS, HEAD_DIM),
      jnp.tile(qk_g[0], reps).reshape(1, cw), jnp.tile(qk_g[1], reps).reshape(1, cw))
    return list(res[:4]) + [list(res[4 + n * n_d:4 + (n + 1) * n_d]) for n in range(3)]


def _inproj_odd_kernel(x_ref, g_ref, w_ref, bd_ref, gq_ref, gk_ref, *rest, kv_block):
    if kv_block:
        bias_ref, qb_ref, ka_ref, kb_ref, kf_ref, vf_ref, vt_ref = rest
    else:
        qb_ref, kf_ref, vf_ref = rest
    x = x_ref[...]
    h = (x * _rms_rows(x) * g_ref[...]).astype(BF16)
    cw = qb_ref.shape[1]

    def col(c):
        return jnp.dot(h, w_ref[:, c * cw:(c + 1) * cw], preferred_element_type=F32)

    q = _group_rmsnorm(col(0), bd_ref, gq_ref[...])
    k = _group_rmsnorm(col(1), bd_ref, gk_ref[...])
    qb_ref[...] = (q * (ATTN_SCALE * LOG2E)).astype(BF16)
    kf_ref[...] = k
    v = col(2)
    vf_ref[...] = v
    if kv_block:
        first_half = lax.broadcasted_iota(jnp.int32, (1, LANES), 1) < HEAD_DIM
        for hh in range(cw // LANES):
            sl = slice(hh * LANES, (hh + 1) * LANES)
            k_h, tab = k[:, sl], bias_ref[hh]
            ka_ref[:, sl] = jnp.where(first_half, k_h, tab).astype(BF16)
            kb_ref[:, sl] = jnp.where(first_half, pltpu.roll(k_h, HEAD_DIM, 1), tab).astype(BF16)
        vt = v.T.astype(BF16)
        for c in range(vt_ref.shape[0]):
            vt_ref[c] = vt[:, c * kv_block:(c + 1) * kv_block]


BIAS_LANES = 3


def _key_bias_table(n_heads, block):
    slope = np.asarray([_alibi_slope(h, n_heads) * LOG2E for h in range(n_heads)], np.float32)
    rest = slope[:, None] * np.arange(block, dtype=np.float32)[None, :]
    tab = np.zeros((n_heads, block, LANES), np.float32)
    for t in range(BIAS_LANES):
        term = (rest.view(np.uint32) & np.uint32(0xFFFF0000)).view(np.float32)
        tab[:, :, HEAD_DIM + t] = term
        rest = rest - term
    return jnp.asarray(tab)


def _inproj_odd(x, g, w, qk_g, *, tm, kv_block=0):
    t, d = x.shape
    cw = w.shape[1] // 3
    reps = cw // HEAD_DIM
    row = lambda i: (i, 0)
    fix = lambda i: (0, 0)
    ins = [x, g.reshape(1, d), w, _group_ones(MXU_COLS, HEAD_DIM),
           jnp.tile(qk_g[0], reps).reshape(1, cw), jnp.tile(qk_g[1], reps).reshape(1, cw)]
    in_specs = [pl.BlockSpec((tm, d), row), pl.BlockSpec((1, d), fix), pl.BlockSpec(w.shape, fix),
                pl.BlockSpec((MXU_COLS, MXU_COLS), fix), pl.BlockSpec((1, cw), fix), pl.BlockSpec((1, cw), fix)]
    tile = pl.BlockSpec((tm, cw), row)
    tok_bf16, tok_f32 = jax.ShapeDtypeStruct((t, cw), BF16), jax.ShapeDtypeStruct((t, cw), F32)
    if kv_block:
        assert tm == kv_block
        bias = _key_bias_table(cw // LANES, kv_block)
        ins.append(bias)
        in_specs.append(pl.BlockSpec(bias.shape, lambda i: (0, 0, 0)))
        outs = [tok_bf16, tok_bf16, tok_bf16, tok_f32, tok_f32,
                jax.ShapeDtypeStruct((t // kv_block, cw, kv_block), BF16)]
        out_specs = [tile] * 5 + [pl.BlockSpec((tm // kv_block, cw, kv_block), lambda i: (i, 0, 0))]
    else:
        outs = [tok_bf16, tok_f32, tok_f32]
        out_specs = [tile] * 3
    return pl.pallas_call(
        functools.partial(_inproj_odd_kernel, kv_block=kv_block),
        out_shape=outs,
        grid=(t // tm,),
        in_specs=in_specs,
        out_specs=out_specs,
        compiler_params=_params("parallel"),
        name="inproj_odd",
    )(*ins)


def _outproj_kernel(x_ref, a_ref, w_ref, o_ref):
    o_ref[...] = x_ref[...] + jnp.dot(a_ref[...].astype(BF16), w_ref[...], preferred_element_type=F32)


def _outproj(x, a, w, *, tm):
    t, d = x.shape
    row = lambda i: (i, 0)
    return pl.pallas_call(
        _outproj_kernel,
        out_shape=jax.ShapeDtypeStruct((t, d), F32),
        grid=(t // tm,),
        in_specs=[pl.BlockSpec((tm, d), row), pl.BlockSpec((tm, a.shape[1]), row),
                  pl.BlockSpec(w.shape, lambda i: (0, 0))],
        out_specs=pl.BlockSpec((tm, d), row),
        compiler_params=_params("parallel"),
        name="outproj",
    )(x, a, w)


def _alibi_slope(h, n_heads):
    return 2.0 ** (-8.0 * (h + 1) / n_heads)


def _dilated_prompt_kernel(bias_ref, q_ref, kc_ref, kp_ref, vc_ref, vp_ref, o_ref, lse_ref, *, n_heads):
    step = pl.program_id(2)
    tq = kp_ref.shape[1]
    n_sub = q_ref.shape[1] // tq
    lane_hi = lax.broadcasted_iota(jnp.int32, (1, LANES), 1) >= HEAD_DIM
    for sub in range(n_sub):
        rows = slice(sub * tq, (sub + 1) * tq)
        before = slice((sub - 1) * tq, sub * tq)
        variant = jnp.minimum(step, 1) if sub == 0 else 1
        for p in range(n_heads // 2):
            sl = slice(p * LANES, (p + 1) * LANES)
            q = q_ref[0, rows, sl]
            k_prev, v_prev = (kp_ref[0, :, sl], vp_ref[0, :, sl]) if sub == 0 else (kc_ref[0, before, sl],
                                                                                   vc_ref[0, before, sl])
            k = jnp.concatenate([k_prev, kc_ref[0, rows, sl]], axis=0)
            v = jnp.concatenate([v_prev, vc_ref[0, rows, sl]], axis=0)
            o_pair = jnp.zeros((tq, LANES), F32)
            lse_pair = jnp.zeros((tq, LANES), F32)
            for e in range(2):
                keep = lane_hi if e else jnp.logical_not(lane_hi)
                qe = jnp.where(keep, q, jnp.zeros_like(q))
                ve = jnp.where(keep, v, jnp.zeros_like(v))
                s = _nt(qe, k) + bias_ref[variant, 2 * p + e]
                m = jnp.max(s, axis=-1, keepdims=True)
                pr = jnp.exp(s - m)
                l = jnp.sum(pr, axis=-1, keepdims=True)
                o_pair = o_pair + jnp.dot(pr.astype(BF16), ve, preferred_element_type=F32) * (1.0 / l)
                lse_pair = jnp.where(keep, m + jnp.log(l), lse_pair)
            o_ref[0, rows, sl] = o_pair
            lse_ref[0, rows, sl] = lse_pair


def _dilated_bias_table(n_heads, dilation):
    tq = SUB_WINDOW
    qi = jnp.arange(tq)[:, None]
    kj = jnp.arange(2 * tq)[None, :]
    dist = qi + tq - kj
    band = (dist >= 0) & (dist <= SUB_WINDOW)
    slope = jnp.asarray([_alibi_slope(h, n_heads) for h in range(n_heads)], F32)[:, None, None]
    bias = -slope * (dilation * dist).astype(F32)[None]
    return jnp.stack([jnp.where(band & (kj >= tq), bias, NEG_INF), jnp.where(band, bias, NEG_INF)])


def _dilated_prompt(q, k, v, *, batch, dilation):
    d = dilation
    t, width = q.shape[0] * d, q.shape[1] // d
    s_len = t // batch
    tq = SUB_WINDOW
    n_sub = max(c for c in (1, 2, 4) if s_len % (c * d * tq) == 0)
    steps = s_len // (d * tq * n_sub)
    assert steps * d * tq * n_sub == s_len
    n_heads = width // HEAD_DIM
    bias = _dilated_bias_table(n_heads, d)
    view = lambda a: a.reshape(batch, s_len // d, d * width)
    cur = lambda b, r, i: (b, i, r)
    prev = lambda b, r, i: (b, jnp.maximum(i * n_sub - 1, 0), r)
    blk, halo = (1, n_sub * tq, width), (1, tq, width)
    o, lse = pl.pallas_call(
        functools.partial(_dilated_prompt_kernel, n_heads=n_heads),
        out_shape=[jax.ShapeDtypeStruct((batch, s_len // d, d * width), F32)] * 2,
        grid=(batch, d, steps),
        in_specs=[pl.BlockSpec(bias.shape, lambda b, r, i: (0, 0, 0, 0)),
                  pl.BlockSpec(blk, cur), pl.BlockSpec(blk, cur), pl.BlockSpec(halo, prev),
                  pl.BlockSpec(blk, cur), pl.BlockSpec(halo, prev)],
        out_specs=[pl.BlockSpec(blk, cur)] * 2,
        compiler_params=_params("parallel", "parallel", "arbitrary"),
        name=f"dilated_prompt_d{d}",
    )(bias, view(q), view(k), view(k), view(v), view(v))
    return o.reshape(t // d, d * width), lse.reshape(t // d, d * width)


def _even_out_kernel(x_ref, bg_ref, u_ref, *rest, seq_rows, sample):
    if sample:
        s1_ref, s2_ref = rest[:2]
        rest = rest[2:]
    else:
        halo_ref = rest[0]
        rest = rest[1:]
    tm, cw = u_ref.shape
    if sample:
        o1_ref, cw_ref, w_ref, out_ref = rest
    else:
        n_d = len(DILATIONS)
        branch_refs, (cw_ref, w_ref, out_ref), stages = rest[:2 * n_d], rest[2 * n_d:2 * n_d + 3], rest[2 * n_d + 3:]

        def token_rows(n):
            d, g_ref = DILATIONS[n % n_d], branch_refs[n]
            if d == 1:
                return g_ref[...]
            stage = stages[n]
            for r in range(d):
                for c in range(cw // LANES):
                    lanes = slice(r * cw + c * LANES, r * cw + (c + 1) * LANES)
                    stage[c, pl.ds(r, tm // d, stride=d), :] = g_ref[:, lanes]
            return jnp.concatenate([stage[c] for c in range(cw // LANES)], axis=1)
    u = u_ref[...]
    r1 = pltpu.roll(u, 1, 0)
    r2 = pltpu.roll(u, 2, 0)
    if sample:
        t_in_seq = lax.broadcasted_iota(jnp.int32, (tm, cw), 0) % seq_rows
        u1 = jnp.where(t_in_seq >= 1, r1, s1_ref[...])
        u2 = jnp.where(t_in_seq >= 2, r2, s2_ref[...])
    else:
        first = (pl.program_id(0) % (seq_rows // tm)) == 0
        halo = jnp.where(first, 0.0, halo_ref[...])
        row8 = lax.broadcasted_iota(jnp.int32, halo.shape, 0)
        top1 = jnp.where(row8 < 1, pltpu.roll(halo, 1, 0), r1[0:8])
        top2 = jnp.where(row8 < 2, pltpu.roll(halo, 2, 0), r2[0:8])
        u1 = jnp.concatenate([top1, r1[8:]], axis=0)
        u2 = jnp.concatenate([top2, r2[8:]], axis=0)
    cwt = cw_ref[...]
    a = bg_ref[...] * (cwt[0:1] * u2 + cwt[1:2] * u1 + cwt[2:3] * u)
    if sample:
        b = o1_ref[...]
    else:
        o1, o2, o3, l1, l2, l3 = (token_rows(n) for n in range(2 * n_d))
        mx = jnp.maximum(jnp.maximum(l1, l2), l3)
        e1, e2, e3 = jnp.exp(l1 - mx), jnp.exp(l2 - mx), jnp.exp(l3 - mx)
        b = (e1 * o1 + e2 * o2 + e3 * o3) * (1.0 / (e1 + e2 + e3))
    out_ref[...] = (x_ref[...]
                    + jnp.dot(a.astype(BF16), w_ref[0:cw, :], preferred_element_type=F32)
                    + jnp.dot(b.astype(BF16), w_ref[cw:, :], preferred_element_type=F32))


def _even_out(x, bg, u, hist, outs, lses, conv_w, w_out, *, tm, seq_rows, sample):
    t, d = x.shape
    cw = u.shape[1]
    row = lambda i: (i, 0)
    fix = lambda i: (0, 0)
    tile = pl.BlockSpec((tm, cw), row)
    if sample:
        hist_in, hist_specs = list(hist), [tile, tile]
        branch_specs, scratch = [tile], []
    else:
        hist_in = [u]
        hist_specs = [pl.BlockSpec((8, cw), lambda i: (jnp.maximum(i * (tm // 8) - 1, 0), 0))]
        branch_specs = [pl.BlockSpec((tm // dil, dil * cw), row) for dil in DILATIONS] * 2
        scratch = [pltpu.VMEM((cw // LANES, tm, LANES), F32)] * (2 * len(DILATIONS))
    return pl.pallas_call(
        functools.partial(_even_out_kernel, seq_rows=seq_rows, sample=sample),
        out_shape=jax.ShapeDtypeStruct((t, d), F32),
        grid=(t // tm,),
        in_specs=[pl.BlockSpec((tm, d), row), tile, tile] + hist_specs + branch_specs
                 + [pl.BlockSpec(conv_w.shape, fix), pl.BlockSpec(w_out.shape, fix)],
        out_specs=pl.BlockSpec((tm, d), row),
        scratch_shapes=scratch,
        compiler_params=_params("parallel"),
        name="even_out",
    )(x, bg, u, *hist_in, *outs, *lses, conv_w, w_out)


def _diff_prompt_kernel(lam_ref, sub_ref, slope_ref, q_ref, ka_ref, kb_ref, vt_ref, o_ref,
                        qs_ref, sa_ref, sb_ref, ma_ref, mb_ref, m_ref, l_ref, acc_ref, *, lam_init):
    i = pl.program_id(2)
    tq = q_ref.shape[1]
    n_grp = 2 * tq // MXU_COLS
    q = q_ref[0].astype(F32)
    lane = lax.broadcasted_iota(jnp.int32, (1, LANES), 1)
    ones = jnp.where((lane >= HEAD_DIM) & (lane < HEAD_DIM + BIAS_LANES), 1.0, 0.0)
    qs_ref[0:tq, :] = jnp.where(lane < HEAD_DIM, q, ones).astype(BF16)
    qs_ref[tq:, :] = jnp.where(lane < HEAD_DIM, pltpu.roll(q, HEAD_DIM, 1), ones).astype(BF16)
    m_ref[...] = jnp.full_like(m_ref, NEG_INF)
    l_ref[...] = jnp.zeros_like(l_ref)
    acc_ref[...] = jnp.zeros_like(acc_ref)
    slope = slope_ref[0][:, 0:1]
    key_row = lax.broadcasted_iota(jnp.int32, (tq, LANES), 0)
    q_col = lax.broadcasted_iota(jnp.int32, (tq, LANES), 1)

    buf_a, buf_b = (sa_ref, ma_ref), (sb_ref, mb_ref)

    def score(j, buf):
        s_ref, smax_ref = buf
        rows = pl.ds(pl.multiple_of(j * tq, tq), tq)
        for c in range(n_grp):
            cols = slice(c * MXU_COLS, (c + 1) * MXU_COLS)
            k_ref = ka_ref if c < n_grp // 2 else kb_ref
            s = _nt(k_ref[0, rows, :], qs_ref[cols, :])
            s_ref[:, cols] = s
            smax_ref[:, cols] = jnp.max(s, axis=0, keepdims=True)

    def absorb(j, buf, diagonal):
        s_ref, smax_ref = buf
        vt = vt_ref[j]
        off = slope * ((j - i) * tq).astype(F32)
        m_all, l_all = m_ref[...], l_ref[...]
        m_out, l_out = [], []
        for c in range(n_grp):
            cols = slice(c * MXU_COLS, (c + 1) * MXU_COLS)
            s = s_ref[:, cols]
            if diagonal:
                s = jnp.concatenate(
                    [jnp.where(key_row <= q_col + (c * MXU_COLS + cc * LANES) % tq,
                               s[:, cc * LANES:(cc + 1) * LANES], NEG_INF) for cc in range(MXU_COLS // LANES)], axis=1)
                smax = jnp.max(s, axis=0, keepdims=True)
            else:
                smax = smax_ref[:, cols]
            m_prev = m_all[:, cols]
            m_new = jnp.maximum(m_prev, smax + off)
            alpha = jnp.exp2(m_prev - m_new)
            p = jnp.exp2(s - (m_new - off))
            l_out.append(alpha * l_all[:, cols] + jnp.sum(p, axis=0, keepdims=True))
            acc_ref[:, cols] = alpha * acc_ref[:, cols] + jnp.dot(vt, p.astype(BF16), preferred_element_type=F32)
            m_out.append(m_new)
        m_ref[...] = jnp.concatenate(m_out, axis=1)
        l_ref[...] = jnp.concatenate(l_out, axis=1)

    score(0, buf_a)

    def pair(jj, carry):
        j = 2 * jj
        score(j + 1, buf_b)
        absorb(j, buf_a, False)
        score(j + 2, buf_a)
        absorb(j + 1, buf_b, False)
        return carry

    lax.fori_loop(0, i // 2, pair, 0)

    @pl.when(i % 2 == 1)
    def _():
        score(i, buf_b)
        absorb(i - 1, buf_a, False)
        absorb(i, buf_b, True)

    @pl.when(i % 2 == 0)
    def _():
        absorb(i, buf_a, True)

    lm = lam_ref[...]
    lam = (jnp.exp(jnp.sum(lm[0:1] * lm[1:2], axis=-1, keepdims=True))
           - jnp.exp(jnp.sum(lm[2:3] * lm[3:4], axis=-1, keepdims=True)) + lam_init)
    acc = acc_ref[...]
    inv = 1.0 / l_ref[...]
    o_t = acc[:, :tq] * inv[:, :tq] - lam * (acc[:, tq:] * inv[:, tq:])
    o = o_t.T
    o = o * _rms_rows(o) * sub_ref[...] * (1.0 - lam_init)
    o_ref[0] = o.astype(BF16)


def _diff_prompt(q, ka, kb, vt, lam, subln, *, batch, lam_init, tq):
    t, width = q.shape
    s_len = t // batch
    hw = 2 * HEAD_DIM
    n_heads = width // hw
    nq = s_len // tq
    assert vt.shape == (t // tq, width, tq) and nq * tq == s_len
    slopes = jnp.asarray([_alibi_slope(h, n_heads) * LOG2E for h in range(n_heads)], F32)
    slope_tab = jnp.broadcast_to(slopes[:, None, None], (n_heads, 1, LANES))
    out = pl.pallas_call(
        functools.partial(_diff_prompt_kernel, lam_init=lam_init),
        out_shape=jax.ShapeDtypeStruct((batch, s_len, width), BF16),
        grid=(batch, n_heads, nq),
        in_specs=[
            pl.BlockSpec(lam.shape, lambda b, h, i: (0, 0)),
            pl.BlockSpec((1, hw), lambda b, h, i: (0, 0)),
            pl.BlockSpec((1, 1, LANES), lambda b, h, i: (h, 0, 0)),
            pl.BlockSpec((1, tq, hw), lambda b, h, i: (b, i, h)),
            pl.BlockSpec((1, s_len, hw), lambda b, h, i: (b, 0, h)),
            pl.BlockSpec((1, s_len, hw), lambda b, h, i: (b, 0, h)),
            pl.BlockSpec((nq, hw, tq), lambda b, h, i: (b, h, 0)),
        ],
        out_specs=pl.BlockSpec((1, tq, hw), lambda b, h, i: (b, i, h)),
        scratch_shapes=[pltpu.VMEM((2 * tq, hw), BF16), pltpu.VMEM((tq, 2 * tq), F32), pltpu.VMEM((tq, 2 * tq), F32)]
                       + [pltpu.VMEM((1, 2 * tq), F32)] * 4 + [pltpu.VMEM((hw, 2 * tq), F32)],
        compiler_params=_params("parallel", "parallel", "arbitrary"),
        name="diff_prompt",
    )(lam, subln.reshape(1, hw), slope_tab, *(a.reshape(batch, s_len, width) for a in (q, ka, kb)), vt)
    return out.reshape(t, width)


SAMPLE_ROWS = 16


def _dilated_sample_tables(n_tok, w_buf):
    def tables(dist, valid):
        count = sum(((dist % d == 0) & (dist <= d * SUB_WINDOW)).astype(F32) for d in DILATIONS)
        held = valid & (count > 0)
        return (jnp.where(held, dist.astype(F32), -NEG_INF),
                jnp.where(held, jnp.log(jnp.maximum(count, 1.0)), 0.0))

    t = jnp.arange(SAMPLE_ROWS)[:, None]
    j = jnp.arange(LANES)[None, :]
    cache = tables(w_buf + t - jnp.arange(w_buf)[None, :], t < n_tok)
    new = tables(t - j, (t < n_tok) & (j <= t))
    return cache + new


def _dilated_sample_kernel(dc_ref, lc_ref, dn_ref, ln_ref, q_ref, kn_ref, vn_ref, kt_ref, vt_ref, o_ref):
    n_seq, n_heads, n_tok, hd = q_ref.shape
    pad_q = jnp.zeros((SAMPLE_ROWS - n_tok, hd), F32)
    pad_n = jnp.zeros((LANES - n_tok, hd), F32)
    rmax = lambda a: jnp.max(a, axis=-1, keepdims=True)
    rsum = lambda a: jnp.sum(a, axis=-1, keepdims=True)
    for b in range(n_seq):
        for h in range(n_heads):
            slope = _alibi_slope(h, n_heads)
            q = jnp.concatenate([q_ref[b, h], pad_q], axis=0).astype(BF16)
            kn = jnp.concatenate([kn_ref[b, h], pad_n], axis=0).astype(BF16)
            vn = jnp.concatenate([vn_ref[b, h], pad_n], axis=0).astype(BF16)
            s = jnp.dot(q, kt_ref[0, b, h].astype(BF16), preferred_element_type=F32)
            a = s - slope * dc_ref[...] + lc_ref[...]
            an = _nt(q, kn) - slope * dn_ref[...] + ln_ref[...]
            m = jnp.maximum(rmax(a), rmax(an))
            e = jnp.exp(a - m)
            en = jnp.exp(an - m)
            o = _nt(e.astype(BF16), vt_ref[0, b, h].astype(BF16)) + jnp.dot(en.astype(BF16), vn,
                                                                            preferred_element_type=F32)
            o_ref[b, h] = (o * (1.0 / (rsum(e) + rsum(en))))[0:n_tok]


def _dilated_sample(q, k_new, v_new, cache_kt, cache_vt, layer_j, *, n_tok):
    n, n_heads, tok_pad, hd = q.shape
    w_buf = cache_kt.shape[-1]
    assert n_tok <= min(DILATIONS[1:]) and w_buf == max(DILATIONS) * SUB_WINDOW
    tabs = _dilated_sample_tables(n_tok, w_buf)
    n_seq = 2 if n % 2 == 0 else 1
    fix = lambda i: (0, 0)
    tok = pl.BlockSpec((n_seq, n_heads, tok_pad, hd), lambda i: (i, 0, 0, 0))
    blk = pl.BlockSpec((1, n_seq, n_heads, hd, w_buf), lambda i: (layer_j, i, 0, 0, 0))
    return pl.pallas_call(
        _dilated_sample_kernel,
        out_shape=jax.ShapeDtypeStruct((n, n_heads, tok_pad, hd), F32),
        grid=(n // n_seq,),
        in_specs=[pl.BlockSpec(tb.shape, fix) for tb in tabs] + [tok, tok, tok, blk, blk],
        out_specs=tok,
        compiler_params=_params("parallel"),
        name="dilated_sample",
    )(*tabs, q, k_new, v_new, cache_kt, cache_vt)


def _diff_sample_tables(n_tok, n_heads):
    r = jnp.arange(2 * n_tok * n_heads)[:, None]
    t, h = (r // n_heads) % n_tok, r % n_heads
    slope = (jnp.exp2(-8.0 * (h + 1) / n_heads) * LOG2E).astype(F32)
    c = jnp.arange(PAGE_SIZE * n_heads)[None, :]
    tab_page = jnp.where(c % n_heads == h, slope * (c // n_heads - PAGE_SIZE).astype(F32), NEG_INF)
    cn = jnp.arange(LANES)[None, :]
    j = cn // n_heads
    tab_new = jnp.where((cn < n_tok * n_heads) & (cn % n_heads == h) & (j <= t), slope * j.astype(F32), NEG_INF)
    return tab_page, tab_new, slope


def _diff_sample_kernel(pt_ref, lam_ref, sub_ref, bp_ref, bn_ref, slope_ref, q_ref, kn_ref, vn_ref, *rest,
                        lam_init, group, n_pages):
    k_refs, v_refs = rest[:group], rest[group:2 * group]
    o_ref, w_ref, m_ref, l_ref, acc_ref = rest[2 * group:]
    g = pl.program_id(1)
    n_tok, n_heads, hw = q_ref.shape[1:]
    rows = n_tok * n_heads
    flat = lambda ref: ref[...].reshape(-1, hw)

    @pl.when(g == 0)
    def _():
        q = flat(q_ref)
        lane_hi = lax.broadcasted_iota(jnp.int32, (1, hw), 1) >= HEAD_DIM
        w_ref[0:rows, :] = jnp.where(lane_hi, 0.0, q).astype(BF16)
        w_ref[rows:, :] = jnp.where(lane_hi, q, 0.0).astype(BF16)
        m_ref[...] = jnp.full_like(m_ref, NEG_INF)
        l_ref[...] = jnp.zeros_like(l_ref)
        acc_ref[...] = jnp.zeros_like(acc_ref)

    def absorb(scores, offsets, values):
        m_prev = m_ref[...]
        m_new = m_prev
        for s, off in zip(scores, offsets):
            m_new = jnp.maximum(m_new, jnp.max(s, axis=-1, keepdims=True) + off)
        alpha = jnp.exp2(m_prev - m_new)
        l = alpha * l_ref[...]
        acc = alpha * acc_ref[...]
        for s, off, v in zip(scores, offsets, values):
            p = jnp.exp2(s - (m_new - off))
            l = l + jnp.sum(p, axis=-1, keepdims=True)
            acc = acc + jnp.dot(p.astype(BF16), v, preferred_element_type=F32)
        l_ref[...] = l
        acc_ref[...] = acc
        m_ref[...] = m_new

    slope = slope_ref[...]
    w = w_ref[...]
    absorb([_nt(w, flat(k_refs[i]).astype(BF16)) + bp_ref[...] for i in range(group)],
           [slope * ((g * group + i + 1 - n_pages) * PAGE_SIZE).astype(F32) for i in range(group)],
           [flat(v_refs[i]).astype(BF16) for i in range(group)])

    @pl.when(g == pl.num_programs(1) - 1)
    def _():
        pad = jnp.zeros((LANES - rows, hw), F32)
        kn = jnp.concatenate([flat(kn_ref), pad], axis=0).astype(BF16)
        vn = jnp.concatenate([flat(vn_ref), pad], axis=0).astype(BF16)
        absorb([_nt(w, kn) + bn_ref[...]], [0.0], [vn])
        lm = lam_ref[...]
        lam = (jnp.exp(jnp.sum(lm[0:1] * lm[1:2], axis=-1, keepdims=True))
               - jnp.exp(jnp.sum(lm[2:3] * lm[3:4], axis=-1, keepdims=True)) + lam_init)
        o = acc_ref[...] * (1.0 / l_ref[...])
        o = o[0:rows] - lam * o[rows:]
        o = o * _rms_rows(o) * sub_ref[...] * (1.0 - lam_init)
        o_ref[0] = o.reshape(n_tok, n_heads, hw)


def _diff_sample(q, k_new, v_new, cache_k, cache_v, page_table, lam, subln, layer_j, *, lam_init, group):
    n, n_tok, n_heads, hw = q.shape
    n_pages = page_table.shape[1]
    assert n_pages % group == 0
    rows = 2 * n_tok * n_heads
    tab_page, tab_new, slope = _diff_sample_tables(n_tok, n_heads)
    fix = lambda i, g, pt: (0, 0)
    tok = pl.BlockSpec((1, n_tok, n_heads, hw), lambda i, g, pt: (i, 0, 0, 0))

    def page_spec(slot):
        return pl.BlockSpec((1, 1, PAGE_SIZE, n_heads, hw),
                            lambda i, g, pt: (layer_j, pt[i, g * group + slot], 0, 0, 0))

    pages = [page_spec(s) for s in range(group)]
    return pl.pallas_call(
        functools.partial(_diff_sample_kernel, lam_init=lam_init, group=group, n_pages=n_pages),
        out_shape=jax.ShapeDtypeStruct((n, n_tok, n_heads, hw), F32),
        grid_spec=pltpu.PrefetchScalarGridSpec(
            num_scalar_prefetch=1,
            grid=(n, n_pages // group),
            in_specs=[pl.BlockSpec(lam.shape, fix), pl.BlockSpec((1, hw), fix), pl.BlockSpec(tab_page.shape, fix),
                      pl.BlockSpec(tab_new.shape, fix), pl.BlockSpec(slope.shape, fix), tok, tok, tok]
                     + pages + pages,
            out_specs=tok,
            scratch_shapes=[pltpu.VMEM((rows, hw), BF16), pltpu.VMEM((rows, 1), F32),
                            pltpu.VMEM((rows, 1), F32), pltpu.VMEM((rows, hw), F32)],
        ),
        compiler_params=_params("parallel", "arbitrary"),
        name="diff_sample",
    )(page_table, lam, subln.reshape(1, hw), tab_page, tab_new, slope, q, k_new, v_new,
      *([cache_k] * group), *([cache_v] * group))


def _by_head(a, *, shp):
    n_tok = shp[1]
    a = a.reshape(shp).transpose(0, 2, 1, 3)
    return jnp.pad(a, ((0, 0), (0, 0), (0, -n_tok % 8), (0, 0)))


def _row_tile(rows, target):
    tile = min(rows, target)
    while rows % tile:
        tile //= 2
    return tile


def kernel(x_prompt, x_sample, state_conv, cache_b_k, cache_b_v, cache_c_k, cache_c_v, page_table, norm_gain,
           ffn_w_gate, ffn_w_up, ffn_w_down, w_in_even, conv_w, qk_gain_b, w_out_even, w_in_odd, qk_gain_c,
           lambda_c, subln_c, w_out_odd):
    batch, seq, d = x_prompt.shape
    n, n_tok, _ = x_sample.shape
    depth = norm_gain.shape[0]
    w_buf = cache_b_k.shape[2]
    xp = x_prompt.reshape(batch * seq, d)
    xs = x_sample.reshape(n * n_tok, d)
    tp, ts = _row_tile(batch * seq, 512), _row_tile(n * n_tok, 512)
    tq = _row_tile(seq, 512)
    wg, wu, wd = ffn_w_gate.astype(BF16), ffn_w_up.astype(BF16), ffn_w_down.astype(BF16)
    conv_p, conv_s, bk_p, bv_p, bk_s, bv_s, ck_p, cv_p, ck_s, cv_s = ([] for _ in range(10))

    for layer in range(depth):
        g = norm_gain[layer]
        j = layer // 2
        xp = _half_ffn(xp, g[0], wg, wu, wd, (layer, 0), tm=tp)
        xs = _half_ffn(xs, g[0], wg, wu, wd, (layer, 0), tm=ts)
        proj_p = proj_s = ()
        if layer % 2 == 0:
            w_in, w_out = w_in_even[j].astype(BF16), w_out_even[j].astype(BF16)
            cw = w_in.shape[1] // 6
            heads = (cw // HEAD_DIM, HEAD_DIM)
            bg, u, kf, vf, qs, ks, vs = _inproj_even(xp, g[1], w_in, qk_gain_b[j], tm=tp, dilations=DILATIONS)
            res = [_dilated_prompt(qs[b], ks[b], vs[b], batch=batch, dilation=dil) for b, dil in enumerate(DILATIONS)]
            xp = _even_out(xp, bg, u, None, [r[0] for r in res], [r[1] for r in res], conv_w[j], w_out,
                           tm=tp, seq_rows=seq, sample=False)
            conv_p.append(u.reshape(batch, seq, cw)[:, seq - (CONV_K - 1):])
            last = lambda a: a.reshape(batch, seq, cw)[:, seq - w_buf:].reshape(batch, w_buf, *heads)
            bk_p.append(last(kf))
            bv_p.append(last(vf))

            bg, u, kf, vf, (qb,), _, _ = _inproj_even(xs, g[1], w_in, qk_gain_b[j], tm=ts)
            shp = (n, n_tok) + heads
            by_head = functools.partial(_by_head, shp=shp)
            o = _dilated_sample(by_head(qb.astype(F32)), by_head(kf), by_head(vf),
                                cache_b_k.transpose(0, 1, 3, 4, 2), cache_b_v.transpose(0, 1, 3, 4, 2), j,
                                n_tok=n_tok)
            o = o[:, :, :n_tok].transpose(0, 2, 1, 3)
            st = state_conv[j]
            hist2 = jnp.pad(st, ((0, 0), (0, n_tok - (CONV_K - 1)), (0, 0))).reshape(n * n_tok, cw)
            hist1 = jnp.pad(st[:, 1:], ((0, 0), (0, n_tok - 1), (0, 0))).reshape(n * n_tok, cw)
            xs = _even_out(xs, bg, u, (hist1, hist2), [o.reshape(n * n_tok, cw)], [], conv_w[j], w_out,
                           tm=ts, seq_rows=n_tok, sample=True)
            conv_s.append(u.reshape(n, n_tok, cw)[:, n_tok - (CONV_K - 1):])
            bk_s.append(kf.reshape(shp))
            bv_s.append(vf.reshape(shp))
        else:
            w_in, w_out = w_in_odd[j].astype(BF16), w_out_odd[j].astype(BF16)
            cw = w_in.shape[1] // 3
            heads = (cw // (2 * HEAD_DIM), 2 * HEAD_DIM)
            lam_init = 0.8 - 0.6 * math.exp(-0.3 * layer)
            qb, ka, kb, kf, vf, vt = _inproj_odd(xp, g[1], w_in, qk_gain_c[j], tm=tq, kv_block=tq)
            o = _diff_prompt(qb, ka, kb, vt, lambda_c[j], subln_c[j], batch=batch, lam_init=lam_init, tq=tq)
            proj_p = (o, w_out)
            ck_p.append(kf.reshape(batch, seq // PAGE_SIZE, PAGE_SIZE, *heads))
            cv_p.append(vf.reshape(batch, seq // PAGE_SIZE, PAGE_SIZE, *heads))

            qb, kf, vf = _inproj_odd(xs, g[1], w_in, qk_gain_c[j], tm=ts)
            shp = (n, n_tok) + heads
            o = _diff_sample(qb.astype(F32).reshape(shp), kf.reshape(shp), vf.reshape(shp), cache_c_k, cache_c_v,
                             page_table, lambda_c[j], subln_c[j], j, lam_init=lam_init,
                             group=math.gcd(page_table.shape[1], 16))
            proj_s = (o.reshape(n * n_tok, cw), w_out)
            ck_s.append(kf.reshape(shp))
            cv_s.append(vf.reshape(shp))
        xp = _half_ffn(xp, g[2], wg, wu, wd, (layer, 1), tm=tp, proj=proj_p)
        xs = _half_ffn(xs, g[2], wg, wu, wd, (layer, 1), tm=ts, proj=proj_s)

    return (xp.reshape(batch, seq, d), xs.reshape(n, n_tok, d),
            jnp.stack(conv_p), jnp.stack(conv_s),
            jnp.stack(bk_p), jnp.stack(bv_p), jnp.stack(bk_s), jnp.stack(bv_s),
            jnp.stack(ck_p), jnp.stack(cv_p), jnp.stack(ck_s), jnp.stack(cv_s))
```
